```python
import numpy as np
import jax, jax.numpy as jnp
from jax import lax

D_MODEL = 1024
BATCH = 8
SEQ = 4096
DEPTH = 4

N_MIXERS = 2
N_META = 16
GRID_W = 64
NA_HEADS = 16
NA_HEAD_DIM = D_MODEL // NA_HEADS
NA_WIN_H_MAX = 8
NA_WIN_W = 16
NA_KEY_BLOCK_W = 2 * NA_WIN_W
CONV_WIDTH = 31
D_FF = 4 * D_MODEL
N_CONV_LAYERS = (DEPTH + 1) // 2
N_NA_LAYERS = DEPTH // 2
EPS = 1e-6
NEG = -1e30

kernel_name = "hybrid_conv_natten_encoder"


def rms_norm(x, g):
    xf = x.astype(jnp.float32)
    y = xf * lax.rsqrt(jnp.mean(xf * xf, axis=-1, keepdims=True) + EPS)
    return (y * g.astype(jnp.float32)).astype(x.dtype)


def layer_norm(x, g, b):
    xf = x.astype(jnp.float32)
    mu = jnp.mean(xf, axis=-1, keepdims=True)
    xc = xf - mu
    y = xc * lax.rsqrt(jnp.mean(xc * xc, axis=-1, keepdims=True) + EPS)
    return (y * g.astype(jnp.float32) + b.astype(jnp.float32)).astype(x.dtype)


def conv_module(h, w_in, b_in, w_dw, b_dw, ln_g, ln_b, w_out, b_out):
    u = h @ w_in + b_in
    a, gate = jnp.split(u, 2, axis=-1)
    u = a * jax.nn.sigmoid(gate)
    u = lax.conv_general_dilated(
        u, w_dw[:, None, :].astype(u.dtype), window_strides=(1,),
        padding=[(CONV_WIDTH // 2, CONV_WIDTH // 2)],
        dimension_numbers=("NWC", "WIO", "NWC"),
        feature_group_count=D_MODEL) + b_dw
    u = jax.nn.silu(layer_norm(u, ln_g, ln_b))
    return u @ w_out + b_out


def _column_tables():
    n_cb = GRID_W // NA_WIN_W
    qc = np.arange(GRID_W).reshape(n_cb, NA_WIN_W)
    ks = np.clip(np.arange(n_cb) * NA_WIN_W - NA_WIN_W // 2, 0, GRID_W - NA_KEY_BLOCK_W)
    kc = ks[:, None] + np.arange(NA_KEY_BLOCK_W)
    sc = np.clip(qc - NA_WIN_W // 2, 0, GRID_W - NA_WIN_W)
    kcb = kc[:, None, :]
    valid = (kcb >= sc[..., None]) & (kcb < sc[..., None] + NA_WIN_W)
    dc = np.clip(kcb - qc[..., None], -(NA_WIN_W - 1), NA_WIN_W - 1) + NA_WIN_W - 1
    return [int(s) for s in ks], valid, dc


def neighbourhood_attention(h, w_qkv, w_o, rpb):
    B, L, _ = h.shape
    T = L - N_META
    rows = T // GRID_W
    kh = min(NA_WIN_H_MAX, rows)
    dt = h.dtype
    qkv = (h @ w_qkv).reshape(B, L, 3, NA_HEADS, NA_HEAD_DIM)
    q = jnp.moveaxis(qkv[:, :, 0], 1, 2) * (NA_HEAD_DIM ** -0.5)
    k = jnp.moveaxis(qkv[:, :, 1], 1, 2)
    v = jnp.moveaxis(qkv[:, :, 2], 1, 2)
    qm, km, vm = q[:, :, :N_META], k[:, :, :N_META], v[:, :, :N_META]
    grid = (B, NA_HEADS, rows, GRID_W, NA_HEAD_DIM)
    qg = q[:, :, N_META:].reshape(grid)
    kg = k[:, :, N_META:].reshape(grid)
    vg = v[:, :, N_META:].reshape(grid)

    s_mm = jnp.einsum("bhqd,bhkd->bhqk", qm, km).astype(jnp.float32)
    out_meta = jnp.einsum("bhqk,bhkd->bhqd", jax.nn.softmax(s_mm, axis=-1).astype(dt), vm)

    ks, valid, dc = _column_tables()
    n_cb = len(ks)
    mask_bias = jnp.where(jnp.asarray(valid), 0.0, NEG).astype(jnp.float32)
    mask_bias = mask_bias[None, None, :, :, None, :]

    def row_fn(r):
        sr = jnp.clip(r - kh // 2, 0, rows - kh)
        q_row = lax.dynamic_index_in_dim(qg, r, axis=2, keepdims=False)
        q_row = q_row.reshape(B, NA_HEADS, n_cb, NA_WIN_W, NA_HEAD_DIM)
        k_band = lax.dynamic_slice_in_dim(kg, sr, kh, axis=2)
        v_band = lax.dynamic_slice_in_dim(vg, sr, kh, axis=2)
        k_blk = jnp.stack([k_band[:, :, :, s:s + NA_KEY_BLOCK_W] for s in ks], axis=2)
        v_blk = jnp.stack([v_band[:, :, :, s:s + NA_KEY_BLOCK_W] for s in ks], axis=2)
        dr = sr + jnp.arange(kh) - r + NA_WIN_H_MAX - 1
        bias = jnp.take(rpb, dr, axis=1).astype(jnp.float32)[:, :, dc]
        bias = jnp.transpose(bias, (0, 2, 3, 1, 4))
        s_win = jnp.einsum("bhnqd,bhnikd->bhnqik", q_row, k_blk).astype(jnp.float32)
        s_win = (s_win + bias[None] + mask_bias).reshape(B, NA_HEADS, n_cb, NA_WIN_W, kh * NA_KEY_BLOCK_W)
        s_meta = jnp.einsum("bhnqd,bhmd->bhnqm", q_row, km).astype(jnp.float32)
        p = jax.nn.softmax(jnp.concatenate([s_meta, s_win], axis=-1), axis=-1)
        p_meta = p[..., :N_META].astype(dt)
        p_win = p[..., N_META:].reshape(B, NA_HEADS, n_cb, NA_WIN_W, kh, NA_KEY_BLOCK_W).astype(dt)
        o = (jnp.einsum("bhnqm,bhmd->bhnqd", p_meta, vm)
             + jnp.einsum("bhnqik,bhnikd->bhnqd", p_win, v_blk))
        return o.reshape(B, NA_HEADS, GRID_W, NA_HEAD_DIM)

    out_grid = lax.map(row_fn, jnp.arange(rows))
    out_grid = jnp.transpose(out_grid, (1, 2, 0, 3, 4)).reshape(B, NA_HEADS, T, NA_HEAD_DIM)
    o = jnp.concatenate([out_meta, out_grid], axis=2)
    o = jnp.moveaxis(o, 1, 2).reshape(B, L, NA_HEADS * NA_HEAD_DIM)
    return o @ w_o


def setup_inputs(seed: int = 0) -> dict:
    key = jax.random.key(seed)
    ks = jax.random.split(key, 20)
    D = D_MODEL
    nrm = jax.random.normal
    return {
        "x": nrm(ks[0], (BATCH, SEQ, D), jnp.float32),
        "meta_tokens": nrm(ks[1], (N_META, D), jnp.float32),
        "norm_mix_g": 1.0 + 0.02 * nrm(ks[2], (DEPTH, D), jnp.float32),
        "norm_mlp_g": 1.0 + 0.02 * nrm(ks[3], (DEPTH, D), jnp.float32),
        "conv_w_in": nrm(ks[4], (N_CONV_LAYERS, D, 2 * D), jnp.float32) * D ** -0.5,
        "conv_b_in": 0.01 * nrm(ks[5], (N_CONV_LAYERS, 2 * D), jnp.float32),
        "conv_w_dw": nrm(ks[6], (N_CONV_LAYERS, CONV_WIDTH, D), jnp.float32) * CONV_WIDTH ** -0.5,
        "conv_b_dw": 0.01 * nrm(ks[7], (N_CONV_LAYERS, D), jnp.float32),
        "conv_ln_g": 1.0 + 0.02 * nrm(ks[8], (N_CONV_LAYERS, D), jnp.float32),
        "conv_ln_b": 0.01 * nrm(ks[9], (N_CONV_LAYERS, D), jnp.float32),
        "conv_w_out": nrm(ks[10], (N_CONV_LAYERS, D, D), jnp.float32) * D ** -0.5,
        "conv_b_out": 0.01 * nrm(ks[11], (N_CONV_LAYERS, D), jnp.float32),
        "na_w_qkv": nrm(ks[12], (N_NA_LAYERS, D, 3 * D), jnp.float32) * D ** -0.5,
        "na_w_o": nrm(ks[13], (N_NA_LAYERS, D, D), jnp.float32) * D ** -0.5,
        "na_rpb": 0.02 * nrm(ks[14], (N_NA_LAYERS, NA_HEADS, 2 * NA_WIN_H_MAX - 1, 2 * NA_WIN_W - 1), jnp.float32),
        "mlp_w1": nrm(ks[15], (DEPTH, D, D_FF), jnp.float32) * D ** -0.5,
        "mlp_w2": nrm(ks[16], (DEPTH, D_FF, D), jnp.float32) * D_FF ** -0.5,
        "final_norm_g": 1.0 + 0.02 * nrm(ks[17], (D,), jnp.float32),
    }


def reference(x, meta_tokens, norm_mix_g, norm_mlp_g, conv_w_in, conv_b_in, conv_w_dw, conv_b_dw,
              conv_ln_g, conv_ln_b, conv_w_out, conv_b_out, na_w_qkv, na_w_o, na_rpb,
              mlp_w1, mlp_w2, final_norm_g):
    B = x.shape[0]
    meta = jnp.broadcast_to(meta_tokens.astype(x.dtype)[None], (B, N_META, D_MODEL))
    h = jnp.concatenate([meta, x], axis=1)
    for i in range(DEPTH):
        j = i // N_MIXERS
        hn = rms_norm(h, norm_mix_g[i])
        if i % N_MIXERS == 0:
            h = h + conv_module(hn, conv_w_in[j], conv_b_in[j], conv_w_dw[j], conv_b_dw[j],
                                conv_ln_g[j], conv_ln_b[j], conv_w_out[j], conv_b_out[j])
        else:
            h = h + neighbourhood_attention(hn, na_w_qkv[j], na_w_o[j], na_rpb[j])
        hn = rms_norm(h, norm_mlp_g[i])
        h = h + jnp.square(jax.nn.relu(hn @ mlp_w1[i])) @ mlp_w2[i]
    h = rms_norm(h, final_norm_g)
    return h[:, N_META:]
```

```python
import functools

import jax
import jax.numpy as jnp
import numpy as np
from jax import lax
from jax.experimental import pallas as pl
from jax.experimental.pallas import tpu as pltpu

D = 1024
B = 8
S = 4096
NM = 16
LP = S + NM
NT = B * LP
DEPTH = 4
GW = 64
ROWS = S // GW
NH = 16
DH = D // NH
WIN_H = 8
WIN_W = 16
CONV_W = 31
CONV_PAD = CONV_W // 2
DFF = 4 * D
EPS = 1e-6
NEG = -1e30

F32 = jnp.float32
BF16 = jnp.bfloat16

SUBLANES = 8
LANES = 128
VMEM_LIMIT = 56 * 1024 * 1024

TM = 512
HALO = 16
FEAT_TILES = D // LANES

QR = 4
QB = QR * GW
NQB = ROWS // QR
BAND_ROWS = QR + WIN_H - 1
BAND = BAND_ROWS * GW
KSLOTS = 768
assert BAND + NM <= KSLOTS and KSLOTS % 256 == 0


def _const_spec(shape):
    nd = len(shape)
    return pl.BlockSpec(shape, lambda *_: (0,) * nd, pipeline_mode=pl.Buffered(1))


def _rms(x, g):
    return x * lax.rsqrt(jnp.mean(x * x, axis=-1, keepdims=True) + EPS) * g


def _conv_in_kernel(x_ref, g_ref, w_ref, b_ref, u_ref):
    hn = _rms(x_ref[...], g_ref[...]).astype(BF16)
    y = jnp.dot(hn, w_ref[...], preferred_element_type=F32) + b_ref[...]
    u = y[:, :D] * jax.nn.sigmoid(y[:, D:])
    for j in range(FEAT_TILES):
        u_ref[pl.ds(j, TM, stride=FEAT_TILES), :] = u[:, j * LANES:(j + 1) * LANES]


def _conv_in(hf, g, w_in, b_in):
    return pl.pallas_call(
        _conv_in_kernel,
        grid=(pl.cdiv(NT, TM),),
        in_specs=[
            pl.BlockSpec((TM, D), lambda i: (i, 0)),
            _const_spec((1, D)),
            _const_spec((D, 2 * D)),
            _const_spec((1, 2 * D)),
        ],
        out_specs=pl.BlockSpec((TM * FEAT_TILES, LANES), lambda i: (i, 0)),
        out_shape=jax.ShapeDtypeStruct((NT * FEAT_TILES, LANES), F32),
        compiler_params=pltpu.CompilerParams(
            dimension_semantics=("arbitrary",), vmem_limit_bytes=VMEM_LIMIT),
        name="conv_in",
    )(hf, g, w_in, b_in)


def _conv_out_kernel(left_ref, main_ref, right_ref, h_ref, wdw_ref, bdw_ref,
                     lng_ref, lnb_ref, wout_ref, bout_ref, *rest,
                     rows, left_valid, right_valid):
    o_ref, win_ref, y_ref = rest[-3:]
    ft = FEAT_TILES
    i = pl.program_id(1)
    lv = left_valid(i)
    rv = right_valid(i)
    win_ref[pl.ds(0, HALO * ft), :] = jnp.where(lv, left_ref[...], 0.0)
    win_ref[pl.ds(HALO * ft, rows * ft), :] = main_ref[...]
    win_ref[pl.ds((HALO + rows) * ft, HALO * ft), :] = jnp.where(rv, right_ref[...], 0.0)

    bias = bdw_ref[...]
    first = HALO - CONV_PAD

    def chunk(c, carry):
        t0 = c * SUBLANES
        accs = [bias] * SUBLANES
        for m in range(SUBLANES + CONV_W - 1):
            wv = win_ref[pl.ds(pl.multiple_of((t0 + first + m) * ft, ft), ft), :]
            for r in range(SUBLANES):
                k = m - r
                if 0 <= k < CONV_W:
                    accs[r] = accs[r] + wdw_ref[k] * wv
        for r in range(SUBLANES):
            y_ref[pl.ds(pl.multiple_of((t0 + r) * ft, ft), ft), :] = accs[r]
        return carry

    lax.fori_loop(0, rows // SUBLANES, chunk, 0)

    y = jnp.concatenate(
        [y_ref[pl.ds(j, rows, stride=ft), :] for j in range(ft)], axis=-1)
    mu = jnp.mean(y, axis=-1, keepdims=True)
    yc = y - mu
    yn = yc * lax.rsqrt(jnp.mean(yc * yc, axis=-1, keepdims=True) + EPS)
    yn = yn * lng_ref[...] + lnb_ref[...]
    act = (yn * jax.nn.sigmoid(yn)).astype(BF16)
    o_ref[...] = (h_ref[...] + bout_ref[...]
                  + jnp.dot(act, wout_ref[...], preferred_element_type=F32))


def _conv_out(u, h3, wdw, bdw, lng, lnb, w_out, b_out):
    ft = FEAT_TILES
    n_main = S // TM
    halo_per_tile = TM // HALO
    meta_blk = S // HALO

    weights = [
        _const_spec((CONV_W, SUBLANES, LANES)),
        _const_spec((SUBLANES, LANES)),
        _const_spec((1, D)),
        _const_spec((1, D)),
        _const_spec((D, D)),
        _const_spec((1, D)),
    ]
    params = pltpu.CompilerParams(
        dimension_semantics=("arbitrary", "arbitrary"), vmem_limit_bytes=VMEM_LIMIT)

    main = pl.pallas_call(
        functools.partial(_conv_out_kernel, rows=TM,
                          left_valid=lambda i: i >= 0,
                          right_valid=lambda i: i < n_main - 1),
        grid=(B, n_main),
        in_specs=[
            pl.BlockSpec((None, HALO * ft, LANES),
                         lambda b, i: (b, jnp.where(i == 0, meta_blk, i * halo_per_tile - 1), 0)),
            pl.BlockSpec((None, TM * ft, LANES), lambda b, i: (b, i, 0)),
            pl.BlockSpec((None, HALO * ft, LANES),
                         lambda b, i: (b, jnp.where(i == n_main - 1, 0, (i + 1) * halo_per_tile), 0)),
            pl.BlockSpec((None, TM, D), lambda b, i: (b, i, 0)),
        ] + weights,
        out_specs=pl.BlockSpec((None, TM, D), lambda b, i: (b, i, 0)),
        out_shape=jax.ShapeDtypeStruct((B, LP, D), F32),
        scratch_shapes=[pltpu.VMEM(((TM + 2 * HALO) * ft, LANES), F32),
                        pltpu.VMEM((TM * ft, LANES), F32)],
        compiler_params=params,
        name="conv_out",
    )(u, u, u, h3, wdw, bdw, lng, lnb, w_out, b_out)

    return pl.pallas_call(
        functools.partial(_conv_out_kernel, rows=NM,
                          left_valid=lambda i: i < 0,
                          right_valid=lambda i: i >= 0),
        grid=(B, 1),
        in_specs=[
            pl.BlockSpec((None, HALO * ft, LANES), lambda b, i: (b, 0, 0)),
            pl.BlockSpec((None, NM * ft, LANES), lambda b, i: (b, meta_blk, 0)),
            pl.BlockSpec((None, HALO * ft, LANES), lambda b, i: (b, 0, 0)),
            pl.BlockSpec((None, NM, D), lambda b, i: (b, meta_blk, 0)),
        ] + weights + [pl.BlockSpec(memory_space=pl.ANY)],
        out_specs=pl.BlockSpec((None, NM, D), lambda b, i: (b, meta_blk, 0)),
        out_shape=jax.ShapeDtypeStruct((B, LP, D), F32),
        scratch_shapes=[pltpu.VMEM(((NM + 2 * HALO) * ft, LANES), F32),
                        pltpu.VMEM((NM * ft, LANES), F32)],
        input_output_aliases={10: 0},
        compiler_params=params,
        name="conv_out_meta",
    )(u, u, u, h3, wdw, bdw, lng, lnb, w_out, b_out, main)


def _mlp_kernel(*refs, with_proj, with_final):
    refs = list(refs)
    h_ref = refs.pop(0)
    if with_proj:
        o_ref, wo_ref = refs.pop(0), refs.pop(0)
    g_ref, w1_ref, w2_ref = refs.pop(0), refs.pop(0), refs.pop(0)
    if with_final:
        gf_ref = refs.pop(0)
    out_ref, a_ref = refs

    h = h_ref[...]
    if with_proj:
        h = h + jnp.dot(o_ref[...], wo_ref[...], preferred_element_type=F32)
    hn = _rms(h, g_ref[...]).astype(BF16)
    for c in range(DFF // D):
        h1 = jnp.dot(hn, w1_ref[:, c * D:(c + 1) * D], preferred_element_type=F32)
        a_ref[:, c * D:(c + 1) * D] = jnp.square(jnp.maximum(h1, 0.0)).astype(BF16)
    out = h + jnp.dot(a_ref[...], w2_ref[...], preferred_element_type=F32)
    if with_final:
        out = _rms(out, gf_ref[...])
    out_ref[...] = out


def _mlp(h, g, w1, w2, *, proj=None, final_g=None):
    with_proj = proj is not None
    with_final = final_g is not None
    if with_final:
        grid = (B, S // TM)
        tok = lambda width: pl.BlockSpec((None, TM, width), lambda b, i: (b, i, 0))
        out_shape = jax.ShapeDtypeStruct((B, S, D), F32)
        sem = ("arbitrary", "arbitrary")
    else:
        grid = (pl.cdiv(NT, TM),)
        tok = lambda width: pl.BlockSpec((TM, width), lambda i: (i, 0))
        out_shape = jax.ShapeDtypeStruct((NT, D), F32)
        sem = ("arbitrary",)
    args, specs = [h], [tok(D)]
    if with_proj:
        o, w_o = proj
        args += [o, w_o]
        specs += [tok(D), _const_spec((D, D))]
    args += [g, w1, w2]
    specs += [_const_spec((1, D)), _const_spec((D, DFF)), _const_spec((DFF, D))]
    if with_final:
        args.append(final_g)
        specs.append(_const_spec((1, D)))
    return pl.pallas_call(
        functools.partial(_mlp_kernel, with_proj=with_proj, with_final=with_final),
        grid=grid,
        in_specs=specs,
        out_specs=tok(D),
        out_shape=out_shape,
        scratch_shapes=[pltpu.VMEM((TM, DFF), BF16)],
        compiler_params=pltpu.CompilerParams(
            dimension_semantics=sem, vmem_limit_bytes=VMEM_LIMIT),
        name="mlp_final" if with_final else ("proj_mlp" if with_proj else "mlp"),
    )(*args)


def _qkv_kernel(x_ref, g_ref, w_ref, q_ref, k_ref, v_ref):
    hn = _rms(x_ref[...], g_ref[...]).astype(BF16)
    for idx, (o_ref, scale) in enumerate(((q_ref, DH ** -0.5), (k_ref, None), (v_ref, None))):
        y = jnp.dot(hn, w_ref[:, idx * D:(idx + 1) * D], preferred_element_type=F32)
        if scale is not None:
            y = y * scale
        o_ref[...] = y.astype(BF16)


def _qkv(hf, g, w_qkv):
    tok = pl.BlockSpec((TM, D), lambda i: (i, 0))
    return pl.pallas_call(
        _qkv_kernel,
        grid=(pl.cdiv(NT, TM),),
        in_specs=[tok, _const_spec((1, D)), _const_spec((D, 3 * D))],
        out_specs=[tok, tok, tok],
        out_shape=[jax.ShapeDtypeStruct((NT, D), BF16)] * 3,
        compiler_params=pltpu.CompilerParams(
            dimension_semantics=("arbitrary",), vmem_limit_bytes=VMEM_LIMIT),
        name="qkv",
    )(hf, g, w_qkv)


def _band_start(j):
    return jnp.clip(j * QR - WIN_H // 2, 0, ROWS - BAND_ROWS)


def _block_type_rows(t, i):
    band_minus_query_row = (-i, -(WIN_H // 2) - i, -(BAND_ROWS - QR) - i)[t]
    first_visible = (0, i, BAND_ROWS - WIN_H)[t]
    return first_visible, band_minus_query_row + WIN_H - 1


def _attn_kernel(rpb_ref, q_ref, k_ref, v_ref, o_ref, tbl_ref, bm_ref, kop_ref, vop_ref):
    hp = pl.program_id(0)
    b = pl.program_id(1)
    lane = lax.broadcasted_iota(jnp.int32, (GW, LANES), 1)
    qcol = lax.broadcasted_iota(jnp.int32, (GW, LANES), 0)
    kcol = lane % GW
    dcol = kcol - qcol + (WIN_W - 1)
    wstart = jnp.clip(qcol - WIN_W // 2, 0, GW - WIN_W)
    col_ok = (kcol >= wstart) & (kcol < wstart + WIN_W)
    low_half = lane < GW

    @pl.when(b == 0)
    def _build_bias():
        n_dr = 2 * WIN_H - 1
        n_dc = 2 * WIN_W - 1
        for hh in range(2):
            head = hp * 2 + hh

            def build_table(dr, carry):
                base = (head * n_dr + dr) * n_dc
                tbl = jnp.full((GW, LANES), NEG, F32)
                for dc in range(n_dc):
                    tbl = jnp.where(dcol == dc, rpb_ref[base + dc], tbl)
                tbl_ref[hh, dr] = jnp.where(col_ok, tbl, NEG)
                return carry

            lax.fori_loop(0, n_dr, build_table, 0)

            neg = jnp.full((GW, LANES), NEG, F32)
            for t in range(3):
                for i in range(QR):
                    first_ok, dr0 = _block_type_rows(t, i)
                    for lb in range(KSLOTS // LANES):
                        halves = []
                        for bidx in (2 * lb, 2 * lb + 1):
                            if bidx < BAND_ROWS and first_ok <= bidx < first_ok + WIN_H:
                                halves.append(tbl_ref[hh, bidx + dr0])
                            elif bidx == BAND_ROWS:
                                halves.append(jnp.where(kcol < NM, 0.0, NEG).astype(F32))
                            else:
                                halves.append(neg)
                        bm_ref[hh, t, i * GW:(i + 1) * GW, lb * LANES:(lb + 1) * LANES] = (
                            jnp.where(low_half, halves[0], halves[1]))

    zpad = jnp.zeros((KSLOTS - BAND - NM, LANES), BF16)
    kop_ref[BAND:BAND + NM, :] = k_ref[S:S + NM, :]
    vop_ref[BAND:BAND + NM, :] = v_ref[S:S + NM, :]
    kop_ref[BAND + NM:, :] = zpad
    vop_ref[BAND + NM:, :] = zpad

    def attend(q2, kop, vop, bias_of_head):
        lane_q = lax.broadcasted_iota(jnp.int32, q2.shape, 1)
        head_lanes = [lane_q < DH, lane_q >= DH]
        outs = []
        for hh in range(2):
            qh = jnp.where(head_lanes[hh], q2, jnp.zeros_like(q2))
            s = lax.dot_general(qh, kop, (((1,), (1,)), ((), ())),
                                preferred_element_type=F32)
            s = s + bias_of_head(hh)
            m = jnp.max(s, axis=-1, keepdims=True)
            e = jnp.exp(s - m)
            l = jnp.sum(e, axis=-1, keepdims=True)
            o = jnp.dot(e.astype(BF16), vop, preferred_element_type=F32)
            outs.append(o / l)
        return jnp.where(head_lanes[0], outs[0], outs[1])

    def block(j, carry):
        t = jnp.where(j == 0, 0, jnp.where(j == NQB - 1, 2, 1))
        ks = pl.multiple_of(_band_start(j) * GW, GW)
        kop_ref[0:BAND, :] = k_ref[pl.ds(ks, BAND), :]
        vop_ref[0:BAND, :] = v_ref[pl.ds(ks, BAND), :]
        qs = pl.multiple_of(j * QB, QB)
        o = attend(q_ref[pl.ds(qs, QB), :], kop_ref[...], vop_ref[...],
                   lambda hh: bm_ref[hh, t])
        o_ref[pl.ds(qs, QB), :] = o.astype(BF16)
        return carry

    lax.fori_loop(0, NQB, block, 0)

    o = attend(q_ref[S:S + NM, :], k_ref[S:S + NM, :], v_ref[S:S + NM, :],
               lambda hh: 0.0)
    o_ref[S:S + NM, :] = o.astype(BF16)


def _attention(rpb_flat, q, k, v):
    blk = pl.BlockSpec((None, LP, LANES), lambda hp, b: (b, 0, hp))
    return pl.pallas_call(
        _attn_kernel,
        grid=(D // LANES, B),
        in_specs=[pl.BlockSpec(memory_space=pltpu.SMEM), blk, blk, blk],
        out_specs=blk,
        out_shape=jax.ShapeDtypeStruct((B, LP, D), BF16),
        scratch_shapes=[
            pltpu.VMEM((2, 2 * WIN_H - 1, GW, LANES), F32),
            pltpu.VMEM((2, 3, QB, KSLOTS), F32),
            pltpu.VMEM((KSLOTS, LANES), BF16),
            pltpu.VMEM((KSLOTS, LANES), BF16),
        ],
        compiler_params=pltpu.CompilerParams(
            dimension_semantics=("arbitrary", "arbitrary"), vmem_limit_bytes=VMEM_LIMIT),
        name="natten",
    )(rpb_flat, q, k, v)


def kernel(x, meta_tokens, norm_mix_g, norm_mlp_g, conv_w_in, conv_b_in, conv_w_dw, conv_b_dw, conv_ln_g, conv_ln_b, conv_w_out, conv_b_out, na_w_qkv, na_w_o, na_rpb, mlp_w1, mlp_w2, final_norm_g):
    assert x.shape == (B, S, D) and x.dtype == F32
    row = lambda a: a.reshape(1, -1).astype(F32)
    meta = jnp.broadcast_to(meta_tokens.astype(x.dtype)[None], (B, NM, D))
    h = jnp.concatenate([x, meta], axis=1).reshape(NT, D)
    for i in range(DEPTH):
        j = i // 2
        last = i == DEPTH - 1
        if i % 2 == 0:
            u = _conv_in(h, row(norm_mix_g[i]), conv_w_in[j].astype(BF16), row(conv_b_in[j]))
            h = _conv_out(
                u.reshape(B, LP * FEAT_TILES, LANES), h.reshape(B, LP, D),
                conv_w_dw[j].reshape(CONV_W, SUBLANES, LANES),
                conv_b_dw[j].reshape(SUBLANES, LANES),
                row(conv_ln_g[j]), row(conv_ln_b[j]),
                conv_w_out[j].astype(BF16), row(conv_b_out[j])).reshape(NT, D)
            proj = None
        else:
            q, k, v = _qkv(h, row(norm_mix_g[i]), na_w_qkv[j].astype(BF16))
            o = _attention(na_rpb[j].reshape(-1),
                           q.reshape(B, LP, D), k.reshape(B, LP, D), v.reshape(B, LP, D))
            proj = (o if last else o.reshape(NT, D), na_w_o[j].astype(BF16))
        h = _mlp(h.reshape(B, LP, D) if last else h, row(norm_mlp_g[i]),
                 mlp_w1[i].astype(BF16), mlp_w2[i].astype(BF16), proj=proj,
                 final_g=row(final_norm_g) if last else None)
    return h
```

```python
import functools

import jax
import jax.numpy as jnp
import numpy as np
from jax import lax
from jax.experimental import pallas as pl
from jax.experimental.pallas import tpu as pltpu

D = 1024
B = 8
S = 4096
NM = 16
LP = S + NM
NT = B * LP
DEPTH = 4
GW = 64
ROWS = S // GW
NH = 16
DH = D // NH
WIN_H = 8
WIN_W = 16
CONV_W = 31
CONV_PAD = CONV_W // 2
DFF = 4 * D
EPS = 1e-6
NEG = -1e30

F32 = jnp.float32
BF16 = jnp.bfloat16

SUBLANES = 8
LANES = 128
VMEM_LIMIT = 56 * 1024 * 1024

TM = 512
HALO = 16
FEAT_TILES = D // LANES

QR = 4
QB = QR * GW
NQB = ROWS // QR
BAND_ROWS = QR + WIN_H - 1
BAND = BAND_ROWS * GW
KSLOTS = 768
assert BAND + NM <= KSLOTS and KSLOTS % 256 == 0


def _const_spec(shape):
    nd = len(shape)
    return pl.BlockSpec(shape, lambda *_: (0,) * nd, pipeline_mode=pl.Buffered(1))


def _rms(x, g):
    return x * lax.rsqrt(jnp.mean(x * x, axis=-1, keepdims=True) + EPS) * g


def _conv_in_kernel(x_ref, g_ref, w_ref, b_ref, u_ref):
    hn = _rms(x_ref[...], g_ref[...]).astype(BF16)
    y = jnp.dot(hn, w_ref[...], preferred_element_type=F32) + b_ref[...]
    u = y[:, :D] * jax.nn.sigmoid(y[:, D:])
    for j in range(FEAT_TILES):
        u_ref[pl.ds(j, TM, stride=FEAT_TILES), :] = u[:, j * LANES:(j + 1) * LANES]


def _conv_in(hf, g, w_in, b_in):
    return pl.pallas_call(
        _conv_in_kernel,
        grid=(pl.cdiv(NT, TM),),
        in_specs=[
            pl.BlockSpec((TM, D), lambda i: (i, 0)),
            _const_spec((1, D)),
            _const_spec((D, 2 * D)),
            _const_spec((1, 2 * D)),
        ],
        out_specs=pl.BlockSpec((TM * FEAT_TILES, LANES), lambda i: (i, 0)),
        out_shape=jax.ShapeDtypeStruct((NT * FEAT_TILES, LANES), F32),
        compiler_params=pltpu.CompilerParams(
            dimension_semantics=("arbitrary",), vmem_limit_bytes=VMEM_LIMIT),
        name="conv_in",
    )(hf, g, w_in, b_in)


def _conv_out_kernel(left_ref, main_ref, right_ref, h_ref, wdw_ref, bdw_ref,
                     lng_ref, lnb_ref, wout_ref, bout_ref, *rest,
                     rows, left_valid, right_valid):
    o_ref, win_ref, y_ref = rest[-3:]
    ft = FEAT_TILES
    i = pl.program_id(1)
    lv = left_valid(i)
    rv = right_valid(i)
    win_ref[pl.ds(0, HALO * ft), :] = jnp.where(lv, left_ref[...], 0.0)
    win_ref[pl.ds(HALO * ft, rows * ft), :] = main_ref[...]
    win_ref[pl.ds((HALO + rows) * ft, HALO * ft), :] = jnp.where(rv, right_ref[...], 0.0)

    bias = bdw_ref[...]
    first = HALO - CONV_PAD

    def chunk(c, carry):
        t0 = c * SUBLANES
        accs = [bias] * SUBLANES
        for m in range(SUBLANES + CONV_W - 1):
            wv = win_ref[pl.ds(pl.multiple_of((t0 + first + m) * ft, ft), ft), :]
            for r in range(SUBLANES):
                k = m - r
                if 0 <= k < CONV_W:
                    accs[r] = accs[r] + wdw_ref[k] * wv
        for r in range(SUBLANES):
            y_ref[pl.ds(pl.multiple_of((t0 + r) * ft, ft), ft), :] = accs[r]
        return carry

    lax.fori_loop(0, rows // SUBLANES, chunk, 0)

    y = jnp.concatenate(
        [y_ref[pl.ds(j, rows, stride=ft), :] for j in range(ft)], axis=-1)
    mu = jnp.mean(y, axis=-1, keepdims=True)
    yc = y - mu
    yn = yc * lax.rsqrt(jnp.mean(yc * yc, axis=-1, keepdims=True) + EPS)
    yn = yn * lng_ref[...] + lnb_ref[...]
    act = (yn * jax.nn.sigmoid(yn)).astype(BF16)
    o_ref[...] = (h_ref[...] + bout_ref[...]
                  + jnp.dot(act, wout_ref[...], preferred_element_type=F32))


def _conv_out(u, h3, wdw, bdw, lng, lnb, w_out, b_out):
    ft = FEAT_TILES
    n_main = S // TM
    halo_per_tile = TM // HALO
    meta_blk = S // HALO

    weights = [
        _const_spec((CONV_W, SUBLANES, LANES)),
        _const_spec((SUBLANES, LANES)),
        _const_spec((1, D)),
        _const_spec((1, D)),
        _const_spec((D, D)),
        _const_spec((1, D)),
    ]
    params = pltpu.CompilerParams(
        dimension_semantics=("arbitrary", "arbitrary"), vmem_limit_bytes=VMEM_LIMIT)

    main = pl.pallas_call(
        functools.partial(_conv_out_kernel, rows=TM,
                          left_valid=lambda i: i >= 0,
                          right_valid=lambda i: i < n_main - 1),
        grid=(B, n_main),
        in_specs=[
            pl.BlockSpec((None, HALO * ft, LANES),
                         lambda b, i: (b, jnp.where(i == 0, meta_blk, i * halo_per_tile - 1), 0)),
            pl.BlockSpec((None, TM * ft, LANES), lambda b, i: (b, i, 0)),
            pl.BlockSpec((None, HALO * ft, LANES),
                         lambda b, i: (b, jnp.where(i == n_main - 1, 0, (i + 1) * halo_per_tile), 0)),
            pl.BlockSpec((None, TM, D), lambda b, i: (b, i, 0)),
        ] + weights,
        out_specs=pl.BlockSpec((None, TM, D), lambda b, i: (b, i, 0)),
        out_shape=jax.ShapeDtypeStruct((B, LP, D), F32),
        scratch_shapes=[pltpu.VMEM(((TM + 2 * HALO) * ft, LANES), F32),
                        pltpu.VMEM((TM * ft, LANES), F32)],
        compiler_params=params,
        name="conv_out",
    )(u, u, u, h3, wdw, bdw, lng, lnb, w_out, b_out)

    return pl.pallas_call(
        functools.partial(_conv_out_kernel, rows=NM,
                          left_valid=lambda i: i < 0,
                          right_valid=lambda i: i >= 0),
        grid=(B, 1),
        in_specs=[
            pl.BlockSpec((None, HALO * ft, LANES), lambda b, i: (b, 0, 0)),
            pl.BlockSpec((None, NM * ft, LANES), lambda b, i: (b, meta_blk, 0)),
            pl.BlockSpec((None, HALO * ft, LANES), lambda b, i: (b, 0, 0)),
            pl.BlockSpec((None, NM, D), lambda b, i: (b, meta_blk, 0)),
        ] + weights + [pl.BlockSpec(memory_space=pl.ANY)],
        out_specs=pl.BlockSpec((None, NM, D), lambda b, i: (b, meta_blk, 0)),
        out_shape=jax.ShapeDtypeStruct((B, LP, D), F32),
        scratch_shapes=[pltpu.VMEM(((NM + 2 * HALO) * ft, LANES), F32),
                        pltpu.VMEM((NM * ft, LANES), F32)],
        input_output_aliases={10: 0},
        compiler_params=params,
        name="conv_out_meta",
    )(u, u, u, h3, wdw, bdw, lng, lnb, w_out, b_out, main)


def _mlp_kernel(*refs, with_proj, with_final):
    refs = list(refs)
    h_ref = refs.pop(0)
    if with_proj:
        o_ref, wo_ref = refs.pop(0), refs.pop(0)
    g_ref, w1_ref, w2_ref = refs.pop(0), refs.pop(0), refs.pop(0)
    if with_final:
        gf_ref = refs.pop(0)
    out_ref, a_ref = refs

    h = h_ref[...]
    if with_proj:
        h = h + jnp.dot(o_ref[...], wo_ref[...], preferred_element_type=F32)
    hn = _rms(h, g_ref[...]).astype(BF16)
    for c in range(DFF // D):
        h1 = jnp.dot(hn, w1_ref[:, c * D:(c + 1) * D], preferred_element_type=F32)
        a_ref[:, c * D:(c + 1) * D] = jnp.square(jnp.maximum(h1, 0.0)).astype(BF16)
    out = h + jnp.dot(a_ref[...], w2_ref[...], preferred_element_type=F32)
    if with_final:
        out = _rms(out, gf_ref[...])
    out_ref[...] = out


def _mlp(h, g, w1, w2, *, proj=None, final_g=None):
    with_proj = proj is not None
    with_final = final_g is not None
    if with_final:
        grid = (B, S // TM)
        tok = lambda width: pl.BlockSpec((None, TM, width), lambda b, i: (b, i, 0))
        out_shape = jax.ShapeDtypeStruct((B, S, D), F32)
        sem = ("arbitrary", "arbitrary")
    else:
        grid = (pl.cdiv(NT, TM),)
        tok = lambda width: pl.BlockSpec((TM, width), lambda i: (i, 0))
        out_shape = jax.ShapeDtypeStruct((NT, D), F32)
        sem = ("arbitrary",)
    args, specs = [h], [tok(D)]
    if with_proj:
        o, w_o = proj
        args += [o, w_o]
        specs += [tok(D), _const_spec((D, D))]
    args += [g, w1, w2]
    specs += [_const_spec((1, D)), _const_spec((D, DFF)), _const_spec((DFF, D))]
    if with_final:
        args.append(final_g)
        specs.append(_const_spec((1, D)))
    return pl.pallas_call(
        functools.partial(_mlp_kernel, with_proj=with_proj, with_final=with_final),
        grid=grid,
        in_specs=specs,
        out_specs=tok(D),
        out_shape=out_shape,
        scratch_shapes=[pltpu.VMEM((TM, DFF), BF16)],
        compiler_params=pltpu.CompilerParams(
            dimension_semantics=sem, vmem_limit_bytes=VMEM_LIMIT),
        name="mlp_final" if with_final else ("proj_mlp" if with_proj else "mlp"),
    )(*args)


def _qkv_kernel(x_ref, g_ref, w_ref, q_ref, k_ref, v_ref):
    hn = _rms(x_ref[...], g_ref[...]).astype(BF16)
    for idx, (o_ref, scale) in enumerate(((q_ref, DH ** -0.5), (k_ref, None), (v_ref, None))):
        y = jnp.dot(hn, w_ref[:, idx * D:(idx + 1) * D], preferred_element_type=F32)
        if scale is not None:
            y = y * scale
        o_ref[...] = y.astype(BF16)


def _qkv(hf, g, w_qkv):
    tok = pl.BlockSpec((TM, D), lambda i: (i, 0))
    return pl.pallas_call(
        _qkv_kernel,
        grid=(pl.cdiv(NT, TM),),
        in_specs=[tok, _const_spec((1, D)), _const_spec((D, 3 * D))],
        out_specs=[tok, tok, tok],
        out_shape=[jax.ShapeDtypeStruct((NT, D), BF16)] * 3,
        compiler_params=pltpu.CompilerParams(
            dimension_semantics=("arbitrary",), vmem_limit_bytes=VMEM_LIMIT),
        name="qkv",
    )(hf, g, w_qkv)


def _band_start(j):
    return jnp.clip(j * QR - WIN_H // 2, 0, ROWS - BAND_ROWS)


def _block_type_rows(t, i):
    band_minus_query_row = (-i, -(WIN_H // 2) - i, -(BAND_ROWS - QR) - i)[t]
    first_visible = (0, i, BAND_ROWS - WIN_H)[t]
    return first_visible, band_minus_query_row + WIN_H - 1


def _attn_kernel(rpb_ref, q_ref, k_ref, v_ref, o_ref, tbl_ref, bm_ref,
                 kop0, kop1, vop0, vop1, s0, s1, p0, p1, m0, m1, l0, l1):
    kops, vops = (kop0, kop1), (vop0, vop1)
    s_refs, p_refs, m_refs, l_refs = (s0, s1), (p0, p1), (m0, m1), (l0, l1)
    hp = pl.program_id(0)
    b = pl.program_id(1)
    lane = lax.broadcasted_iota(jnp.int32, (GW, LANES), 1)
    qcol = lax.broadcasted_iota(jnp.int32, (GW, LANES), 0)
    kcol = lane % GW
    dcol = kcol - qcol + (WIN_W - 1)
    wstart = jnp.clip(qcol - WIN_W // 2, 0, GW - WIN_W)
    col_ok = (kcol >= wstart) & (kcol < wstart + WIN_W)
    low_half = lane < GW

    @pl.when(b == 0)
    def _build_bias():
        n_dr = 2 * WIN_H - 1
        n_dc = 2 * WIN_W - 1
        for hh in range(2):
            head = hp * 2 + hh

            def build_table(dr, carry):
                base = (head * n_dr + dr) * n_dc
                tbl = jnp.full((GW, LANES), NEG, F32)
                for dc in range(n_dc):
                    tbl = jnp.where(dcol == dc, rpb_ref[base + dc], tbl)
                tbl_ref[hh, dr] = jnp.where(col_ok, tbl, NEG)
                return carry

            lax.fori_loop(0, n_dr, build_table, 0)

            neg = jnp.full((GW, LANES), NEG, F32)
            for t in range(3):
                for i in range(QR):
                    first_ok, dr0 = _block_type_rows(t, i)
                    for lb in range(KSLOTS // LANES):
                        halves = []
                        for bidx in (2 * lb, 2 * lb + 1):
                            if bidx < BAND_ROWS and first_ok <= bidx < first_ok + WIN_H:
                                halves.append(tbl_ref[hh, bidx + dr0])
                            elif bidx == BAND_ROWS:
                                halves.append(jnp.where(kcol < NM, 0.0, NEG).astype(F32))
                            else:
                                halves.append(neg)
                        bm_ref[hh, t, i * GW:(i + 1) * GW, lb * LANES:(lb + 1) * LANES] = (
                            jnp.where(low_half, halves[0], halves[1]))

    zpad = jnp.zeros((KSLOTS - BAND - NM, LANES), BF16)
    for kop_ref, vop_ref in zip(kops, vops):
        kop_ref[BAND:BAND + NM, :] = k_ref[S:S + NM, :]
        vop_ref[BAND:BAND + NM, :] = v_ref[S:S + NM, :]
        kop_ref[BAND + NM:, :] = zpad
        vop_ref[BAND + NM:, :] = zpad

    def head_lanes(shape):
        lane_q = lax.broadcasted_iota(jnp.int32, shape, 1)
        return [lane_q < DH, lane_q >= DH]

    def scores(q2, kop, hh):
        qh = jnp.where(head_lanes(q2.shape)[hh], q2, jnp.zeros_like(q2))
        return lax.dot_general(qh, kop, (((1,), (1,)), ((), ())),
                               preferred_element_type=F32)

    def softmax_terms(s, m):
        e = jnp.exp(s - m)
        return e.astype(BF16), jnp.sum(e, axis=-1, keepdims=True)

    def band_rows(j):
        start = _band_start(j) * GW
        return start if isinstance(j, int) else pl.multiple_of(start, GW)

    def query_rows(j):
        return j * QB if isinstance(j, int) else pl.multiple_of(j * QB, QB)

    def block_type(j):
        if isinstance(j, int):
            return 0 if j == 0 else (2 if j == NQB - 1 else 1)
        return jnp.where(j == NQB - 1, 2, 1)

    def stage_scores(j, slot, hh):
        if hh == 0:
            kops[slot][0:BAND, :] = k_ref[pl.ds(band_rows(j), BAND), :]
        q2 = q_ref[pl.ds(query_rows(j), QB), :]
        t = block_type(j)
        s = scores(q2, kops[slot][...], hh) + bm_ref[hh, t]
        s_refs[slot][hh] = s
        m_refs[slot][hh] = jnp.broadcast_to(jnp.max(s, axis=-1, keepdims=True), (QB, LANES))

    def stage_softmax(slot, hh):
        m = jnp.concatenate([m_refs[slot][hh]] * (KSLOTS // LANES), axis=-1)
        p, l = softmax_terms(s_refs[slot][hh], m)
        p_refs[slot][hh] = p
        l_refs[slot][hh] = jnp.broadcast_to(l, (QB, LANES))

    def stage_values(j, slot, hh):
        if hh == 0:
            vops[slot][0:BAND, :] = v_ref[pl.ds(band_rows(j), BAND), :]
        o = (jnp.dot(p_refs[slot][hh], vops[slot][...], preferred_element_type=F32)
             / l_refs[slot][hh])
        cols = slice(hh * DH, (hh + 1) * DH)
        o_ref[pl.ds(query_rows(j), QB), cols] = o[:, cols].astype(BF16)

    for hh in range(2):
        stage_scores(0, 0, hh)
    for hh in range(2):
        stage_softmax(0, hh)
        stage_scores(1, 1, hh)

    def pair(i, carry):
        j = 2 * i
        for hh in range(2):
            stage_values(j - 2, 0, hh)
            stage_softmax(1, hh)
            stage_scores(j, 0, hh)
        for hh in range(2):
            stage_values(j - 1, 1, hh)
            stage_softmax(0, hh)
            stage_scores(j + 1, 1, hh)
        return carry

    lax.fori_loop(1, NQB // 2, pair, 0)
    for hh in range(2):
        stage_values(NQB - 2, 0, hh)
        stage_softmax(1, hh)
    for hh in range(2):
        stage_values(NQB - 1, 1, hh)

    qm, km, vm = q_ref[S:S + NM, :], k_ref[S:S + NM, :], v_ref[S:S + NM, :]
    outs = []
    for hh in range(2):
        sm = scores(qm, km, hh)
        p, l = softmax_terms(sm, jnp.max(sm, axis=-1, keepdims=True))
        outs.append(jnp.dot(p, vm, preferred_element_type=F32) / l)
    o = jnp.where(head_lanes(outs[0].shape)[0], outs[0], outs[1])
    o_ref[S:S + NM, :] = o.astype(BF16)


def _attention(rpb_flat, q, k, v):
    blk = pl.BlockSpec((None, LP, LANES), lambda hp, b: (b, 0, hp))
    return pl.pallas_call(
        _attn_kernel,
        grid=(D // LANES, B),
        in_specs=[pl.BlockSpec(memory_space=pltpu.SMEM), blk, blk, blk],
        out_specs=blk,
        out_shape=jax.ShapeDtypeStruct((B, LP, D), BF16),
        scratch_shapes=[
            pltpu.VMEM((2, 2 * WIN_H - 1, GW, LANES), F32),
            pltpu.VMEM((2, 3, QB, KSLOTS), F32),
        ] + [pltpu.VMEM((KSLOTS, LANES), BF16)] * 4
          + [pltpu.VMEM((2, QB, KSLOTS), F32)] * 2
          + [pltpu.VMEM((2, QB, KSLOTS), BF16)] * 2
          + [pltpu.VMEM((2, QB, LANES), F32)] * 4,
        compiler_params=pltpu.CompilerParams(
            dimension_semantics=("arbitrary", "arbitrary"), vmem_limit_bytes=VMEM_LIMIT),
        name="natten",
    )(rpb_flat, q, k, v)


def kernel(x, meta_tokens, norm_mix_g, norm_mlp_g, conv_w_in, conv_b_in, conv_w_dw, conv_b_dw, conv_ln_g, conv_ln_b, conv_w_out, conv_b_out, na_w_qkv, na_w_o, na_rpb, mlp_w1, mlp_w2, final_norm_g):
    assert x.shape == (B, S, D) and x.dtype == F32
    row = lambda a: a.reshape(1, -1).astype(F32)
    meta = jnp.broadcast_to(meta_tokens.astype(x.dtype)[None], (B, NM, D))
    h = jnp.concatenate([x, meta], axis=1).reshape(NT, D)
    for i in range(DEPTH):
        j = i // 2
        last = i == DEPTH - 1
        if i % 2 == 0:
            u = _conv_in(h, row(norm_mix_g[i]), conv_w_in[j].astype(BF16), row(conv_b_in[j]))
            h = _conv_out(
                u.reshape(B, LP * FEAT_TILES, LANES), h.reshape(B, LP, D),
                conv_w_dw[j].reshape(CONV_W, SUBLANES, LANES),
                conv_b_dw[j].reshape(SUBLANES, LANES),
                row(conv_ln_g[j]), row(conv_ln_b[j]),
                conv_w_out[j].astype(BF16), row(conv_b_out[j])).reshape(NT, D)
            proj = None
        else:
            q, k, v = _qkv(h, row(norm_mix_g[i]), na_w_qkv[j].astype(BF16))
            o = _attention(na_rpb[j].reshape(-1),
                           q.reshape(B, LP, D), k.reshape(B, LP, D), v.reshape(B, LP, D))
            proj = (o if last else o.reshape(NT, D), na_w_o[j].astype(BF16))
        h = _mlp(h.reshape(B, LP, D) if last else h, row(norm_mlp_g[i]),
                 mlp_w1[i].astype(BF16), mlp_w2[i].astype(BF16), proj=proj,
                 final_g=row(final_norm_g) if last else None)
    return h
```

```python
import functools

import jax
import jax.numpy as jnp
from jax import lax
from jax.experimental import pallas as pl
from jax.experimental.pallas import tpu as pltpu

D = 1024
B = 8
S = 4096
NM = 16
LP = S + NM
NT = B * LP
DEPTH = 4
GW = 64
ROWS = S // GW
NH = 16
DH = D // NH
WIN_H = 8
WIN_W = 16
CONV_W = 31
CONV_PAD = CONV_W // 2
DFF = 4 * D
EPS = 1e-6
NEG = -1e30

F32 = jnp.float32
BF16 = jnp.bfloat16

SUBLANES = 8
LANES = 128
VMEM_LIMIT = 56 * 1024 * 1024

TM = 512
HALO = 16
FEAT_TILES = D // LANES
CONV_ROWS = 16
GLU_COLS = 256

QR = 8
NRB = ROWS // QR
QB = QR * GW
BAND_ROWS = QR + WIN_H - 1
COL_GROUPS = ((0, 24), (24, 16), (40, 24))
KEY_COL0 = (0, 16, 32)
KW = 32
GROUP_ROW0 = (0, 192, 320)
BAND = BAND_ROWS * KW
KSLOTS = 512
assert BAND + NM <= KSLOTS and KSLOTS % 256 == 0
assert all(max(c0 - WIN_W // 2, 0) >= k0 and min(c0 + w - 1 + WIN_W // 2, GW - 1) < k0 + KW
           for (c0, w), k0 in zip(COL_GROUPS, KEY_COL0))
LOG2E = 1.4426950408889634
ATT_BATCH = 2


def _const_spec(shape):
    nd = len(shape)
    return pl.BlockSpec(shape, lambda *_: (0,) * nd, pipeline_mode=pl.Buffered(1))


def _layer_spec(shape, layer):
    nd = len(shape)
    return pl.BlockSpec((None,) + tuple(shape), lambda *_: (layer,) + (0,) * nd,
                        pipeline_mode=pl.Buffered(1))


def _rms(x, g):
    return x * lax.rsqrt(jnp.mean(x * x, axis=-1, keepdims=True) + EPS) * g


def _conv_in_kernel(x_ref, g_ref, w_ref, b_ref, u_ref):
    hn = _rms(x_ref[...], g_ref[...]).astype(BF16)
    for c in range(D // GLU_COLS):
        lin = slice(c * GLU_COLS, (c + 1) * GLU_COLS)
        gate = slice(D + c * GLU_COLS, D + (c + 1) * GLU_COLS)
        a = jnp.dot(hn, w_ref[:, lin], preferred_element_type=F32) + b_ref[:, lin]
        gt = jnp.dot(hn, w_ref[:, gate], preferred_element_type=F32) + b_ref[:, gate]
        u = a * jax.nn.sigmoid(gt)
        for jj in range(GLU_COLS // LANES):
            j = c * (GLU_COLS // LANES) + jj
            u_ref[pl.ds(j, TM, stride=FEAT_TILES), :] = u[:, jj * LANES:(jj + 1) * LANES]


def _conv_in(hf, g, w_in, b_in, layer):
    return pl.pallas_call(
        _conv_in_kernel,
        grid=(pl.cdiv(NT, TM),),
        in_specs=[
            pl.BlockSpec((TM, D), lambda i: (i, 0)),
            _const_spec((1, D)),
            _layer_spec((D, 2 * D), layer),
            _const_spec((1, 2 * D)),
        ],
        out_specs=pl.BlockSpec((TM * FEAT_TILES, LANES), lambda i: (i, 0)),
        out_shape=jax.ShapeDtypeStruct((NT * FEAT_TILES, LANES), F32),
        compiler_params=pltpu.CompilerParams(
            dimension_semantics=("arbitrary",), vmem_limit_bytes=VMEM_LIMIT),
        name="conv_in",
    )(hf, g, w_in, b_in)


def _conv_out_kernel(left_ref, main_ref, right_ref, h_ref, wdw_ref, bdw_ref,
                     lng_ref, lnb_ref, wout_ref, bout_ref, o_ref, win_ref, y_ref):
    ft = FEAT_TILES
    n_main = S // TM
    i = pl.program_id(1)
    first = HALO - CONV_PAD

    def conv_rows(rows):
        bias = bdw_ref[...]

        def chunk(c, carry):
            t0 = c * CONV_ROWS
            accs = [bias] * CONV_ROWS
            for m in range(CONV_ROWS + CONV_W - 1):
                wv = win_ref[pl.ds(pl.multiple_of((t0 + first + m) * ft, ft), ft), :]
                for r in range(CONV_ROWS):
                    k = m - r
                    if 0 <= k < CONV_W:
                        accs[r] = accs[r] + wdw_ref[k] * wv
            for r in range(CONV_ROWS):
                y_ref[pl.ds(pl.multiple_of((t0 + r) * ft, ft), ft), :] = accs[r]
            return carry

        lax.fori_loop(0, rows // CONV_ROWS, chunk, 0)

    def finish(rows):
        y = jnp.concatenate(
            [y_ref[pl.ds(j, rows, stride=ft), :] for j in range(ft)], axis=-1)
        mu = jnp.mean(y, axis=-1, keepdims=True)
        yc = y - mu
        yn = yc * lax.rsqrt(jnp.mean(yc * yc, axis=-1, keepdims=True) + EPS)
        yn = yn * lng_ref[...] + lnb_ref[...]
        act = (yn * jax.nn.sigmoid(yn)).astype(BF16)
        o_ref[0:rows, :] = (h_ref[0:rows, :] + bout_ref[...]
                            + jnp.dot(act, wout_ref[...], preferred_element_type=F32))

    zero_halo = jnp.zeros((HALO * ft, LANES), F32)

    @pl.when(i < n_main)
    def _grid_tokens():
        win_ref[pl.ds(0, HALO * ft), :] = left_ref[...]
        win_ref[pl.ds(HALO * ft, TM * ft), :] = main_ref[...]
        win_ref[pl.ds((HALO + TM) * ft, HALO * ft), :] = jnp.where(
            i < n_main - 1, right_ref[...], zero_halo)
        conv_rows(TM)
        finish(TM)

    @pl.when(i == n_main)
    def _meta_tokens():
        win_ref[pl.ds(0, HALO * ft), :] = zero_halo
        win_ref[pl.ds(HALO * ft, NM * ft), :] = main_ref[pl.ds(0, NM * ft), :]
        win_ref[pl.ds((HALO + NM) * ft, HALO * ft), :] = right_ref[...]
        conv_rows(NM)
        finish(NM)


def _conv_out(u, h3, wdw, bdw, lng, lnb, w_out, b_out, layer):
    ft = FEAT_TILES
    n_main = S // TM
    halo_per_tile = TM // HALO
    meta_blk = S // HALO
    return pl.pallas_call(
        _conv_out_kernel,
        grid=(B, n_main + 1),
        in_specs=[
            pl.BlockSpec((None, HALO * ft, LANES),
                         lambda b, i: (b, jnp.where(i == 0, meta_blk, i * halo_per_tile - 1), 0)),
            pl.BlockSpec((None, TM * ft, LANES), lambda b, i: (b, i, 0)),
            pl.BlockSpec((None, HALO * ft, LANES),
                         lambda b, i: (b, jnp.where(i >= n_main - 1, 0, (i + 1) * halo_per_tile), 0)),
            pl.BlockSpec((None, TM, D), lambda b, i: (b, i, 0)),
            _const_spec((CONV_W, SUBLANES, LANES)),
            _const_spec((SUBLANES, LANES)),
            _const_spec((1, D)),
            _const_spec((1, D)),
            _layer_spec((D, D), layer),
            _const_spec((1, D)),
        ],
        out_specs=pl.BlockSpec((None, TM, D), lambda b, i: (b, i, 0)),
        out_shape=jax.ShapeDtypeStruct((B, LP, D), F32),
        scratch_shapes=[pltpu.VMEM(((TM + 2 * HALO) * ft, LANES), F32),
                        pltpu.VMEM((TM * ft, LANES), F32)],
        compiler_params=pltpu.CompilerParams(
            dimension_semantics=("arbitrary", "arbitrary"), vmem_limit_bytes=VMEM_LIMIT),
        name="conv_out",
    )(u, u, u, h3, wdw, bdw, lng, lnb, w_out, b_out)


def _mlp_kernel(*refs, with_proj, with_final):
    refs = list(refs)
    h_ref = refs.pop(0)
    if with_proj:
        o_ref, wo_ref = refs.pop(0), refs.pop(0)
    g_ref, w1_ref, w2_ref = refs.pop(0), refs.pop(0), refs.pop(0)
    if with_final:
        gf_ref = refs.pop(0)
    out_ref, a_ref = refs

    h = h_ref[...]
    if with_proj:
        h = h + jnp.dot(o_ref[...], wo_ref[...], preferred_element_type=F32)
    hn = _rms(h, g_ref[...]).astype(BF16)
    for c in range(DFF // D):
        h1 = jnp.dot(hn, w1_ref[:, c * D:(c + 1) * D], preferred_element_type=F32)
        a_ref[:, c * D:(c + 1) * D] = jnp.square(jnp.maximum(h1, 0.0)).astype(BF16)
    out = h + jnp.dot(a_ref[...], w2_ref[...], preferred_element_type=F32)
    if with_final:
        out = _rms(out, gf_ref[...])
    out_ref[...] = out


def _mlp(h, g, w1, w2, layer, *, proj=None, final_g=None):
    with_proj = proj is not None
    with_final = final_g is not None
    if with_final:
        grid = (B, S // TM)
        tok = lambda width: pl.BlockSpec((None, TM, width), lambda b, i: (b, i, 0))
        out_shape = jax.ShapeDtypeStruct((B, S, D), F32)
        sem = ("arbitrary", "arbitrary")
    else:
        grid = (pl.cdiv(NT, TM),)
        tok = lambda width: pl.BlockSpec((TM, width), lambda i: (i, 0))
        out_shape = jax.ShapeDtypeStruct((NT, D), F32)
        sem = ("arbitrary",)
    args, specs = [h], [tok(D)]
    if with_proj:
        o, w_o, attn_layer = proj
        args += [o, w_o]
        specs += [tok(D), _layer_spec((D, D), attn_layer)]
    args += [g, w1, w2]
    specs += [_const_spec((1, D)), _layer_spec((D, DFF), layer), _layer_spec((DFF, D), layer)]
    if with_final:
        args.append(final_g)
        specs.append(_const_spec((1, D)))
    return pl.pallas_call(
        functools.partial(_mlp_kernel, with_proj=with_proj, with_final=with_final),
        grid=grid,
        in_specs=specs,
        out_specs=tok(D),
        out_shape=out_shape,
        scratch_shapes=[pltpu.VMEM((TM, DFF), BF16)],
        compiler_params=pltpu.CompilerParams(
            dimension_semantics=sem, vmem_limit_bytes=VMEM_LIMIT),
        name="mlp_final" if with_final else ("proj_mlp" if with_proj else "mlp"),
    )(*args)


def _qkv_kernel(x_ref, g_ref, w_ref, q_ref, k_ref, v_ref):
    hn = _rms(x_ref[...], g_ref[...]).astype(BF16)
    for idx, (o_ref, scale) in enumerate(((q_ref, DH ** -0.5 * LOG2E), (k_ref, None), (v_ref, None))):
        y = jnp.dot(hn, w_ref[:, idx * D:(idx + 1) * D], preferred_element_type=F32)
        if scale is not None:
            y = y * scale
        o_ref[...] = y.astype(BF16)


def _qkv(hf, g, w_qkv, layer):
    tok = pl.BlockSpec((TM, D), lambda i: (i, 0))
    return pl.pallas_call(
        _qkv_kernel,
        grid=(pl.cdiv(NT, TM),),
        in_specs=[tok, _const_spec((1, D)), _layer_spec((D, 3 * D), layer)],
        out_specs=[tok, tok, tok],
        out_shape=[jax.ShapeDtypeStruct((NT, D), BF16)] * 3,
        compiler_params=pltpu.CompilerParams(
            dimension_semantics=("arbitrary",), vmem_limit_bytes=VMEM_LIMIT),
        name="qkv",
    )(hf, g, w_qkv)


def _band_start(rb):
    return jnp.clip(rb * QR - WIN_H // 2, 0, ROWS - BAND_ROWS)


def _band_row_bias_index(t, i, br):
    rb = (0, 1, NRB - 1)[t]
    band0 = min(max(rb * QR - WIN_H // 2, 0), ROWS - BAND_ROWS)
    r = rb * QR + i
    sr = min(max(r - WIN_H // 2, 0), ROWS - WIN_H)
    kr = band0 + br
    return kr - r + WIN_H - 1 if sr <= kr < sr + WIN_H else None


def _attn_kernel(rpb_ref, q_ref, k_ref, v_ref, o_ref, tbl_ref, bm_ref, oacc_ref,
                 kop0, kop1, vop0, vop1, s0, s1, p0, p1, m0, m1, l0, l1):
    kops, vops = (kop0, kop1), (vop0, vop1)
    s_refs, p_refs, m_refs, l_refs = (s0, s1), (p0, p1), (m0, m1), (l0, l1)
    hp = pl.program_id(0)
    b = pl.program_id(1)
    n_groups = len(COL_GROUPS)
    lane_tiles = KSLOTS // LANES
    rows_per_tile = LANES // KW

    @pl.when(b == 0)
    def _build_bias():
        n_dr = 2 * WIN_H - 1
        n_dc = 2 * WIN_W - 1
        lane = lax.broadcasted_iota(jnp.int32, (GW, LANES), 1)
        qcol = lax.broadcasted_iota(jnp.int32, (GW, LANES), 0)
        key0 = jnp.where(qcol < COL_GROUPS[1][0], KEY_COL0[0],
                         jnp.where(qcol < COL_GROUPS[2][0], KEY_COL0[1], KEY_COL0[2]))
        kcol = key0 + lane % KW
        dcol = kcol - qcol + (WIN_W - 1)
        wstart = jnp.clip(qcol - WIN_W // 2, 0, GW - WIN_W)
        col_ok = (kcol >= wstart) & (kcol < wstart + WIN_W)
        for hh in range(2):
            head = hp * 2 + hh

            def build_table(dr, carry):
                base = (head * n_dr + dr) * n_dc
                tbl = jnp.full((GW, LANES), NEG, F32)
                for dc in range(n_dc):
                    tbl = jnp.where(dcol == dc, rpb_ref[base + dc] * LOG2E, tbl)
                tbl_ref[hh, dr] = jnp.where(col_ok, tbl, NEG)
                return carry

            lax.fori_loop(0, n_dr, build_table, 0)

            for t in range(3):
                for g, (c0, w) in enumerate(COL_GROUPS):
                    quarter = lax.broadcasted_iota(jnp.int32, (w, LANES), 1) // KW
                    within = lax.broadcasted_iota(jnp.int32, (w, LANES), 1) % KW
                    neg = jnp.full((w, LANES), NEG, F32)
                    for i in range(QR):
                        for lt in range(lane_tiles):
                            tile = neg
                            for qtr in range(rows_per_tile):
                                br = lt * rows_per_tile + qtr
                                if br < BAND_ROWS:
                                    dr = _band_row_bias_index(t, i, br)
                                    if dr is None:
                                        continue
                                    part = tbl_ref[hh, dr, c0:c0 + w, :]
                                else:
                                    part = jnp.where(within < NM, 0.0, NEG).astype(F32)
                                tile = jnp.where(quarter == qtr, part, tile)
                            r0 = GROUP_ROW0[g] + i * w
                            bm_ref[hh, t, r0:r0 + w, lt * LANES:(lt + 1) * LANES] = tile

    zpad = jnp.zeros((KSLOTS - BAND - NM, LANES), BF16)
    for kop_ref, vop_ref in zip(kops, vops):
        for g in range(n_groups):
            kop_ref[g, BAND + NM:, :] = zpad
            vop_ref[g, BAND + NM:, :] = zpad

    def head_lanes(shape):
        lane_q = lax.broadcasted_iota(jnp.int32, shape, 1)
        return [lane_q < DH, lane_q >= DH]

    def scores(q2, kop, hh):
        qh = jnp.where(head_lanes(q2.shape)[hh], q2, jnp.zeros_like(q2))
        return lax.dot_general(qh, kop, (((1,), (1,)), ((), ())),
                               preferred_element_type=F32)

    def softmax_terms(s, m):
        e = jnp.exp2(s - m)
        return e.astype(BF16), jnp.sum(e, axis=-1, keepdims=True)

    def split(u):
        if isinstance(u, int):
            return divmod(u, NRB)
        return u // NRB, u % NRB

    def window_copy(dst_ref, src_ref, u):
        bb, rb = split(u)
        band0 = _band_start(rb) * GW
        for g in range(n_groups):
            for br in range(BAND_ROWS):
                start = band0 + br * GW + KEY_COL0[g]
                if not isinstance(u, int):
                    start = pl.multiple_of(start, 16)
                dst_ref[g, br * KW:(br + 1) * KW, :] = src_ref[bb, pl.ds(start, KW), :]
            dst_ref[g, BAND:BAND + NM, :] = src_ref[bb, S:S + NM, :]

    def query_rows(rb):
        return rb * QB if isinstance(rb, int) else pl.multiple_of(rb * QB, QB)

    def block_type(rb):
        if isinstance(rb, int):
            return 0 if rb == 0 else (2 if rb == NRB - 1 else 1)
        return jnp.where(rb == 0, 0, jnp.where(rb == NRB - 1, 2, 1))

    def group_rows(g):
        return slice(GROUP_ROW0[g], GROUP_ROW0[g] + QR * COL_GROUPS[g][1])

    def stage_scores(t, slot, g, hh, qf):
        c0, w = COL_GROUPS[g]
        q2 = qf[:, c0:c0 + w, :].reshape(QR * w, LANES).astype(BF16)
        s = scores(q2, kops[slot][g], hh) + bm_ref[hh, t, group_rows(g), :]
        s_refs[slot][hh, group_rows(g), :] = s
        m_refs[slot][hh, group_rows(g), :] = jnp.broadcast_to(
            jnp.max(s, axis=-1, keepdims=True), (QR * w, LANES))

    def stage_softmax(slot, g, hh):
        rows = group_rows(g)
        m = jnp.concatenate([m_refs[slot][hh, rows, :]] * lane_tiles, axis=-1)
        p, l = softmax_terms(s_refs[slot][hh, rows, :], m)
        p_refs[slot][hh, rows, :] = p
        l_refs[slot][hh, rows, :] = jnp.broadcast_to(l, (rows.stop - rows.start, LANES))

    def stage_values(slot, g, hh):
        c0, w = COL_GROUPS[g]
        rows = group_rows(g)
        o = (jnp.dot(p_refs[slot][hh, rows, :], vops[slot][g], preferred_element_type=F32)
             / l_refs[slot][hh, rows, :])
        cols = slice(hh * DH, (hh + 1) * DH)
        oacc_ref[:, c0:c0 + w, cols] = o.reshape(QR, w, LANES)[:, :, cols]

    units = [(g, hh) for g in range(n_groups) for hh in range(2)]

    def step(u_values, softmax_slot, u_scores, slot):
        if u_values is not None:
            window_copy(vops[slot], v_ref, u_values)
        if u_scores is not None:
            window_copy(kops[slot], k_ref, u_scores)
            bb, rb = split(u_scores)
            qf = q_ref[bb, pl.ds(query_rows(rb), QB), :].astype(F32).reshape(QR, GW, LANES)
            t = block_type(rb)
        for g, hh in units:
            if u_values is not None:
                stage_values(slot, g, hh)
            if softmax_slot is not None:
                stage_softmax(softmax_slot, g, hh)
            if u_scores is not None:
                stage_scores(t, slot, g, hh, qf)
        if u_values is not None:
            bb, rb = split(u_values)
            o_ref[bb, pl.ds(query_rows(rb), QB), :] = (
                oacc_ref[...].reshape(QB, LANES).astype(BF16))

    n_u = ATT_BATCH * NRB
    step(None, None, 0, 0)
    step(None, 0, 1, 1)

    def pair(i, carry):
        u = 2 * i
        step(u - 2, 1, u, 0)
        step(u - 1, 0, u + 1, 1)
        return carry

    lax.fori_loop(1, n_u // 2, pair, 0)
    step(n_u - 2, 1, None, 0)
    step(n_u - 1, None, None, 1)

    for bb in range(ATT_BATCH):
        qm, km, vm = q_ref[bb, S:S + NM, :], k_ref[bb, S:S + NM, :], v_ref[bb, S:S + NM, :]
        outs = []
        for hh in range(2):
            sm = scores(qm, km, hh)
            p, l = softmax_terms(sm, jnp.max(sm, axis=-1, keepdims=True))
            outs.append(jnp.dot(p, vm, preferred_element_type=F32) / l)
        o = jnp.where(head_lanes(outs[0].shape)[0], outs[0], outs[1])
        o_ref[bb, S:S + NM, :] = o.astype(BF16)


def _attention(rpb_flat, q, k, v):
    blk = pl.BlockSpec((ATT_BATCH, LP, LANES), lambda hp, b: (b, 0, hp))
    return pl.pallas_call(
        _attn_kernel,
        grid=(D // LANES, B // ATT_BATCH),
        in_specs=[pl.BlockSpec(memory_space=pltpu.SMEM), blk, blk, blk],
        out_specs=blk,
        out_shape=jax.ShapeDtypeStruct((B, LP, D), BF16),
        scratch_shapes=[
            pltpu.VMEM((2, 2 * WIN_H - 1, GW, LANES), F32),
            pltpu.VMEM((2, 3, QB, KSLOTS), F32),
            pltpu.VMEM((QR, GW, LANES), F32),
        ] + [pltpu.VMEM((len(COL_GROUPS), KSLOTS, LANES), BF16)] * 4
          + [pltpu.VMEM((2, QB, KSLOTS), F32)] * 2
          + [pltpu.VMEM((2, QB, KSLOTS), BF16)] * 2
          + [pltpu.VMEM((2, QB, LANES), F32)] * 4,
        compiler_params=pltpu.CompilerParams(
            dimension_semantics=("arbitrary", "arbitrary"), vmem_limit_bytes=VMEM_LIMIT),
        name="natten",
    )(rpb_flat, q, k, v)


def kernel(x, meta_tokens, norm_mix_g, norm_mlp_g, conv_w_in, conv_b_in, conv_w_dw, conv_b_dw, conv_ln_g, conv_ln_b, conv_w_out, conv_b_out, na_w_qkv, na_w_o, na_rpb, mlp_w1, mlp_w2, final_norm_g):
    assert x.shape == (B, S, D) and x.dtype == F32
    row = lambda a: a.reshape(1, -1).astype(F32)
    w_in, w_out = conv_w_in.astype(BF16), conv_w_out.astype(BF16)
    w_qkv, w_o = na_w_qkv.astype(BF16), na_w_o.astype(BF16)
    w1, w2 = mlp_w1.astype(BF16), mlp_w2.astype(BF16)
    meta = jnp.broadcast_to(meta_tokens.astype(x.dtype)[None], (B, NM, D))
    h = jnp.concatenate([x, meta], axis=1).reshape(NT, D)
    for i in range(DEPTH):
        j = i // 2
        last = i == DEPTH - 1
        if i % 2 == 0:
            u = _conv_in(h, row(norm_mix_g[i]), w_in, row(conv_b_in[j]), j)
            h = _conv_out(
                u.reshape(B, LP * FEAT_TILES, LANES), h.reshape(B, LP, D),
                conv_w_dw[j].reshape(CONV_W, SUBLANES, LANES),
                conv_b_dw[j].reshape(SUBLANES, LANES),
                row(conv_ln_g[j]), row(conv_ln_b[j]), w_out, row(conv_b_out[j]), j).reshape(NT, D)
            proj = None
        else:
            q, k, v = _qkv(h, row(norm_mix_g[i]), w_qkv, j)
            o = _attention(na_rpb[j].reshape(-1),
                           q.reshape(B, LP, D), k.reshape(B, LP, D), v.reshape(B, LP, D))
            proj = (o if last else o.reshape(NT, D), w_o, j)
        h = _mlp(h.reshape(B, LP, D) if last else h, row(norm_mlp_g[i]), w1, w2, i,
                 proj=proj, final_g=row(final_norm_g) if last else None)
    return h
```

```python
import functools

import jax
import jax.numpy as jnp
from jax import lax
from jax.experimental import pallas as pl
from jax.experimental.pallas import tpu as pltpu

D = 1024
B = 8
S = 4096
NM = 16
LP = S + NM
NT = B * LP
DEPTH = 4
GW = 64
ROWS = S // GW
NH = 16
DH = D // NH
WIN_H = 8
WIN_W = 16
CONV_W = 31
CONV_PAD = CONV_W // 2
DFF = 4 * D
EPS = 1e-6
NEG = -1e30

F32 = jnp.float32
BF16 = jnp.bfloat16

SUBLANES = 8
LANES = 128
VMEM_LIMIT = 56 * 1024 * 1024

TM = 512
HALO = 16
FEAT_TILES = D // LANES
CONV_ROWS = 16
GLU_COLS = 256

QR = 8
NRB = ROWS // QR
QB = QR * GW
BAND_ROWS = QR + WIN_H - 1
COL_GROUPS = ((0, 24), (24, 16), (40, 24))
KEY_COL0 = (0, 16, 32)
KW = 32
GROUP_ROW0 = (0, 192, 320)
BAND = BAND_ROWS * KW
KSLOTS = 512
assert BAND + NM <= KSLOTS and KSLOTS % 256 == 0
assert all(max(c0 - WIN_W // 2, 0) >= k0 and min(c0 + w - 1 + WIN_W // 2, GW - 1) < k0 + KW
           for (c0, w), k0 in zip(COL_GROUPS, KEY_COL0))
LOG2E = 1.4426950408889634
ATT_BATCH = 2
HEAD_PAIRS = D // LANES


def _const_spec(shape):
    nd = len(shape)
    return pl.BlockSpec(shape, lambda *_: (0,) * nd, pipeline_mode=pl.Buffered(1))


def _layer_spec(shape, layer):
    nd = len(shape)
    return pl.BlockSpec((None,) + tuple(shape), lambda *_: (layer,) + (0,) * nd,
                        pipeline_mode=pl.Buffered(1))


def _rms(x, g):
    return x * lax.rsqrt(jnp.mean(x * x, axis=-1, keepdims=True) + EPS) * g


def _glu_rows(x, g_ref, w_ref, b_ref, u_ref):
    rows = x.shape[0]
    hn = _rms(x, g_ref[...]).astype(BF16)
    for c in range(D // GLU_COLS):
        lin = slice(c * GLU_COLS, (c + 1) * GLU_COLS)
        gate = slice(D + c * GLU_COLS, D + (c + 1) * GLU_COLS)
        a = jnp.dot(hn, w_ref[:, lin], preferred_element_type=F32) + b_ref[:, lin]
        gt = jnp.dot(hn, w_ref[:, gate], preferred_element_type=F32) + b_ref[:, gate]
        u = a * jax.nn.sigmoid(gt)
        for jj in range(GLU_COLS // LANES):
            j = c * (GLU_COLS // LANES) + jj
            u_ref[pl.ds(j, rows, stride=FEAT_TILES), :] = u[:, jj * LANES:(jj + 1) * LANES]


def _conv_in_kernel(x_ref, g_ref, w_ref, b_ref, u_ref):
    _glu_rows(x_ref[...], g_ref, w_ref, b_ref, u_ref)


def _conv_in_first_kernel(x_ref, meta_ref, g_ref, w_ref, b_ref, u_ref):
    i = pl.program_id(1)

    @pl.when(i < S // TM)
    def _grid_tokens():
        _glu_rows(x_ref[...], g_ref, w_ref, b_ref, u_ref)

    @pl.when(i == S // TM)
    def _meta_tokens():
        _glu_rows(meta_ref[...], g_ref, w_ref, b_ref, u_ref)


def _conv_in(hf, g, w_in, b_in, layer):
    return pl.pallas_call(
        _conv_in_kernel,
        grid=(pl.cdiv(NT, TM),),
        in_specs=[
            pl.BlockSpec((TM, D), lambda i: (i, 0)),
            _const_spec((1, D)),
            _layer_spec((D, 2 * D), layer),
            _const_spec((1, 2 * D)),
        ],
        out_specs=pl.BlockSpec((TM * FEAT_TILES, LANES), lambda i: (i, 0)),
        out_shape=jax.ShapeDtypeStruct((NT * FEAT_TILES, LANES), F32),
        compiler_params=pltpu.CompilerParams(
            dimension_semantics=("arbitrary",), vmem_limit_bytes=VMEM_LIMIT),
        name="conv_in",
    )(hf, g, w_in, b_in)


def _conv_in_first(x, meta_tokens, g, w_in, b_in, layer):
    n_main = S // TM
    return pl.pallas_call(
        _conv_in_first_kernel,
        grid=(B, n_main + 1),
        in_specs=[
            pl.BlockSpec((None, TM, D), lambda b, i: (b, jnp.minimum(i, n_main - 1), 0)),
            _const_spec((NM, D)),
            _const_spec((1, D)),
            _layer_spec((D, 2 * D), layer),
            _const_spec((1, 2 * D)),
        ],
        out_specs=pl.BlockSpec((None, TM * FEAT_TILES, LANES), lambda b, i: (b, i, 0)),
        out_shape=jax.ShapeDtypeStruct((B, LP * FEAT_TILES, LANES), F32),
        compiler_params=pltpu.CompilerParams(
            dimension_semantics=("arbitrary", "arbitrary"), vmem_limit_bytes=VMEM_LIMIT),
        name="conv_in_first",
    )(x, meta_tokens, g, w_in, b_in)


def _conv_out_kernel(left_ref, main_ref, right_ref, h_ref, hmeta_ref, wdw_ref, bdw_ref,
                     lng_ref, lnb_ref, wout_ref, bout_ref, o_ref, win_ref, y_ref):
    ft = FEAT_TILES
    n_main = S // TM
    i = pl.program_id(1)
    first = HALO - CONV_PAD

    def conv_rows(rows):
        bias = bdw_ref[...]

        def chunk(c, carry):
            t0 = c * CONV_ROWS
            accs = [bias] * CONV_ROWS
            for m in range(CONV_ROWS + CONV_W - 1):
                wv = win_ref[pl.ds(pl.multiple_of((t0 + first + m) * ft, ft), ft), :]
                for r in range(CONV_ROWS):
                    k = m - r
                    if 0 <= k < CONV_W:
                        accs[r] = accs[r] + wdw_ref[k] * wv
            for r in range(CONV_ROWS):
                y_ref[pl.ds(pl.multiple_of((t0 + r) * ft, ft), ft), :] = accs[r]
            return carry

        lax.fori_loop(0, rows // CONV_ROWS, chunk, 0)

    def finish(rows, res_ref):
        y = jnp.concatenate(
            [y_ref[pl.ds(j, rows, stride=ft), :] for j in range(ft)], axis=-1)
        mu = jnp.mean(y, axis=-1, keepdims=True)
        yc = y - mu
        yn = yc * lax.rsqrt(jnp.mean(yc * yc, axis=-1, keepdims=True) + EPS)
        yn = yn * lng_ref[...] + lnb_ref[...]
        act = (yn * jax.nn.sigmoid(yn)).astype(BF16)
        o_ref[0:rows, :] = (res_ref[0:rows, :] + bout_ref[...]
                            + jnp.dot(act, wout_ref[...], preferred_element_type=F32))

    zero_halo = jnp.zeros((HALO * ft, LANES), F32)

    @pl.when(i < n_main)
    def _grid_tokens():
        win_ref[pl.ds(0, HALO * ft), :] = left_ref[...]
        win_ref[pl.ds(HALO * ft, TM * ft), :] = main_ref[...]
        win_ref[pl.ds((HALO + TM) * ft, HALO * ft), :] = jnp.where(
            i < n_main - 1, right_ref[...], zero_halo)
        conv_rows(TM)
        finish(TM, h_ref)

    @pl.when(i == n_main)
    def _meta_tokens():
        win_ref[pl.ds(0, HALO * ft), :] = zero_halo
        win_ref[pl.ds(HALO * ft, NM * ft), :] = main_ref[pl.ds(0, NM * ft), :]
        win_ref[pl.ds((HALO + NM) * ft, HALO * ft), :] = right_ref[...]
        conv_rows(NM)
        finish(NM, hmeta_ref)


def _conv_out(u, res, res_meta, wdw, bdw, lng, lnb, w_out, b_out, layer):
    ft = FEAT_TILES
    n_main = S // TM
    halo_per_tile = TM // HALO
    meta_blk = S // HALO
    if res_meta is None:
        res_meta = res
        meta_spec = pl.BlockSpec((None, NM, D), lambda b, i: (b, S // NM, 0))
    else:
        meta_spec = _const_spec((NM, D))
    return pl.pallas_call(
        _conv_out_kernel,
        grid=(B, n_main + 1),
        in_specs=[
            pl.BlockSpec((None, HALO * ft, LANES),
                         lambda b, i: (b, jnp.where(i == 0, meta_blk, i * halo_per_tile - 1), 0)),
            pl.BlockSpec((None, TM * ft, LANES), lambda b, i: (b, i, 0)),
            pl.BlockSpec((None, HALO * ft, LANES),
                         lambda b, i: (b, jnp.where(i >= n_main - 1, 0, (i + 1) * halo_per_tile), 0)),
            pl.BlockSpec((None, TM, D), lambda b, i: (b, jnp.minimum(i, n_main - 1), 0)),
            meta_spec,
            _const_spec((CONV_W, SUBLANES, LANES)),
            _const_spec((SUBLANES, LANES)),
            _const_spec((1, D)),
            _const_spec((1, D)),
            _layer_spec((D, D), layer),
            _const_spec((1, D)),
        ],
        out_specs=pl.BlockSpec((None, TM, D), lambda b, i: (b, i, 0)),
        out_shape=jax.ShapeDtypeStruct((B, LP, D), F32),
        scratch_shapes=[pltpu.VMEM(((TM + 2 * HALO) * ft, LANES), F32),
                        pltpu.VMEM((TM * ft, LANES), F32)],
        compiler_params=pltpu.CompilerParams(
            dimension_semantics=("arbitrary", "arbitrary"), vmem_limit_bytes=VMEM_LIMIT),
        name="conv_out",
    )(u, u, u, res, res_meta, wdw, bdw, lng, lnb, w_out, b_out)


def _mlp_kernel(*refs, with_proj, with_final):
    refs = list(refs)
    h_ref = refs.pop(0)
    if with_proj:
        o_ref, wo_ref = refs.pop(0), refs.pop(0)
    g_ref, w1_ref, w2_ref = refs.pop(0), refs.pop(0), refs.pop(0)
    if with_final:
        gf_ref = refs.pop(0)
    out_ref, a_ref = refs

    h = h_ref[...]
    if with_proj:
        o = jnp.concatenate([o_ref[hp] for hp in range(HEAD_PAIRS)], axis=-1)
        h = h + jnp.dot(o, wo_ref[...], preferred_element_type=F32)
    hn = _rms(h, g_ref[...]).astype(BF16)
    for c in range(DFF // D):
        h1 = jnp.dot(hn, w1_ref[:, c * D:(c + 1) * D], preferred_element_type=F32)
        a_ref[:, c * D:(c + 1) * D] = jnp.square(jnp.maximum(h1, 0.0)).astype(BF16)
    out = h + jnp.dot(a_ref[...], w2_ref[...], preferred_element_type=F32)
    if with_final:
        out = _rms(out, gf_ref[...])
    out_ref[...] = out


def _mlp(h, g, w1, w2, layer, *, proj=None, final_g=None):
    with_proj = proj is not None
    with_final = final_g is not None
    if with_final:
        grid = (B, S // TM)
        tok = pl.BlockSpec((None, TM, D), lambda b, i: (b, i, 0))
        att = pl.BlockSpec((HEAD_PAIRS, None, TM, LANES), lambda b, i: (0, b, i, 0))
        out_shape = jax.ShapeDtypeStruct((B, S, D), F32)
        sem = ("arbitrary", "arbitrary")
    else:
        grid = (pl.cdiv(NT, TM),)
        tok = pl.BlockSpec((TM, D), lambda i: (i, 0))
        att = pl.BlockSpec((HEAD_PAIRS, TM, LANES), lambda i: (0, i, 0))
        out_shape = jax.ShapeDtypeStruct((NT, D), F32)
        sem = ("arbitrary",)
    args, specs = [h], [tok]
    if with_proj:
        o, w_o, attn_layer = proj
        args += [o, w_o]
        specs += [att, _layer_spec((D, D), attn_layer)]
    args += [g, w1, w2]
    specs += [_const_spec((1, D)), _layer_spec((D, DFF), layer), _layer_spec((DFF, D), layer)]
    if with_final:
        args.append(final_g)
        specs.append(_const_spec((1, D)))
    return pl.pallas_call(
        functools.partial(_mlp_kernel, with_proj=with_proj, with_final=with_final),
        grid=grid,
        in_specs=specs,
        out_specs=tok,
        out_shape=out_shape,
        scratch_shapes=[pltpu.VMEM((TM, DFF), BF16)],
        compiler_params=pltpu.CompilerParams(
            dimension_semantics=sem, vmem_limit_bytes=VMEM_LIMIT),
        name="mlp_final" if with_final else ("proj_mlp" if with_proj else "mlp"),
    )(*args)


def _qkv_kernel(x_ref, g_ref, w_ref, q_ref, k_ref, v_ref):
    hn = _rms(x_ref[...], g_ref[...]).astype(BF16)
    for idx, (o_ref, scale) in enumerate(((q_ref, DH ** -0.5 * LOG2E), (k_ref, None), (v_ref, None))):
        y = jnp.dot(hn, w_ref[:, idx * D:(idx + 1) * D], preferred_element_type=F32)
        if scale is not None:
            y = y * scale
        for hp in range(HEAD_PAIRS):
            o_ref[hp] = y[:, hp * LANES:(hp + 1) * LANES].astype(BF16)


def _qkv(hf, g, w_qkv, layer):
    tok = pl.BlockSpec((TM, D), lambda i: (i, 0))
    out = pl.BlockSpec((HEAD_PAIRS, TM, LANES), lambda i: (0, i, 0))
    return pl.pallas_call(
        _qkv_kernel,
        grid=(pl.cdiv(NT, TM),),
        in_specs=[tok, _const_spec((1, D)), _layer_spec((D, 3 * D), layer)],
        out_specs=[out, out, out],
        out_shape=[jax.ShapeDtypeStruct((HEAD_PAIRS, NT, LANES), BF16)] * 3,
        compiler_params=pltpu.CompilerParams(
            dimension_semantics=("arbitrary",), vmem_limit_bytes=VMEM_LIMIT),
        name="qkv",
    )(hf, g, w_qkv)


def _band_start(rb):
    return jnp.clip(rb * QR - WIN_H // 2, 0, ROWS - BAND_ROWS)


def _band_row_bias_index(t, i, br):
    rb = (0, 1, NRB - 1)[t]
    band0 = min(max(rb * QR - WIN_H // 2, 0), ROWS - BAND_ROWS)
    r = rb * QR + i
    sr = min(max(r - WIN_H // 2, 0), ROWS - WIN_H)
    kr = band0 + br
    return kr - r + WIN_H - 1 if sr <= kr < sr + WIN_H else None


def _attn_kernel(rpb_ref, q_ref, k_ref, v_ref, o_ref, tbl_ref, bm_ref, oacc_ref,
                 kop0, kop1, vop0, vop1, s0, s1, p0, p1, m0, m1, l0, l1):
    kops, vops = (kop0, kop1), (vop0, vop1)
    s_refs, p_refs, m_refs, l_refs = (s0, s1), (p0, p1), (m0, m1), (l0, l1)
    hp = pl.program_id(0)
    b = pl.program_id(1)
    n_groups = len(COL_GROUPS)
    lane_tiles = KSLOTS // LANES
    rows_per_tile = LANES // KW

    @pl.when(b == 0)
    def _build_bias():
        n_dr = 2 * WIN_H - 1
        n_dc = 2 * WIN_W - 1
        lane = lax.broadcasted_iota(jnp.int32, (GW, LANES), 1)
        qcol = lax.broadcasted_iota(jnp.int32, (GW, LANES), 0)
        key0 = jnp.where(qcol < COL_GROUPS[1][0], KEY_COL0[0],
                         jnp.where(qcol < COL_GROUPS[2][0], KEY_COL0[1], KEY_COL0[2]))
        kcol = key0 + lane % KW
        dcol = kcol - qcol + (WIN_W - 1)
        wstart = jnp.clip(qcol - WIN_W // 2, 0, GW - WIN_W)
        col_ok = (kcol >= wstart) & (kcol < wstart + WIN_W)
        for hh in range(2):
            head = hp * 2 + hh

            def build_table(dr, carry):
                base = (head * n_dr + dr) * n_dc
                tbl = jnp.full((GW, LANES), NEG, F32)
                for dc in range(n_dc):
                    tbl = jnp.where(dcol == dc, rpb_ref[base + dc] * LOG2E, tbl)
                tbl_ref[hh, dr] = jnp.where(col_ok, tbl, NEG)
                return carry

            lax.fori_loop(0, n_dr, build_table, 0)

            for t in range(3):
                for g, (c0, w) in enumerate(COL_GROUPS):
                    quarter = lax.broadcasted_iota(jnp.int32, (w, LANES), 1) // KW
                    within = lax.broadcasted_iota(jnp.int32, (w, LANES), 1) % KW
                    neg = jnp.full((w, LANES), NEG, F32)
                    for i in range(QR):
                        for lt in range(lane_tiles):
                            tile = neg
                            for qtr in range(rows_per_tile):
                                br = lt * rows_per_tile + qtr
                                if br < BAND_ROWS:
                                    dr = _band_row_bias_index(t, i, br)
                                    if dr is None:
                                        continue
                                    part = tbl_ref[hh, dr, c0:c0 + w, :]
                                else:
                                    part = jnp.where(within < NM, 0.0, NEG).astype(F32)
                                tile = jnp.where(quarter == qtr, part, tile)
                            r0 = GROUP_ROW0[g] + i * w
                            bm_ref[hh, t, r0:r0 + w, lt * LANES:(lt + 1) * LANES] = tile

    zpad = jnp.zeros((KSLOTS - BAND - NM, LANES), BF16)
    for kop_ref, vop_ref in zip(kops, vops):
        for g in range(n_groups):
            kop_ref[g, BAND + NM:, :] = zpad
            vop_ref[g, BAND + NM:, :] = zpad

    def head_lanes(shape):
        lane_q = lax.broadcasted_iota(jnp.int32, shape, 1)
        return [lane_q < DH, lane_q >= DH]

    def scores(q2, kop, hh):
        qh = jnp.where(head_lanes(q2.shape)[hh], q2, jnp.zeros_like(q2))
        return lax.dot_general(qh, kop, (((1,), (1,)), ((), ())),
                               preferred_element_type=F32)

    def softmax_terms(s, m):
        e = jnp.exp2(s - m)
        return e.astype(BF16), jnp.sum(e, axis=-1, keepdims=True)

    def split(u):
        if isinstance(u, int):
            return divmod(u, NRB)
        return u // NRB, u % NRB

    def window_copy(dst_ref, src_ref, u):
        bb, rb = split(u)
        band0 = _band_start(rb) * GW
        for g in range(n_groups):
            for br in range(BAND_ROWS):
                start = band0 + br * GW + KEY_COL0[g]
                if not isinstance(u, int):
                    start = pl.multiple_of(start, 16)
                dst_ref[g, br * KW:(br + 1) * KW, :] = src_ref[bb, pl.ds(start, KW), :]
            dst_ref[g, BAND:BAND + NM, :] = src_ref[bb, S:S + NM, :]

    def query_rows(rb):
        return rb * QB if isinstance(rb, int) else pl.multiple_of(rb * QB, QB)

    def block_type(rb):
        if isinstance(rb, int):
            return 0 if rb == 0 else (2 if rb == NRB - 1 else 1)
        return jnp.where(rb == 0, 0, jnp.where(rb == NRB - 1, 2, 1))

    def group_rows(g):
        return slice(GROUP_ROW0[g], GROUP_ROW0[g] + QR * COL_GROUPS[g][1])

    def stage_scores(t, slot, g, hh, qf):
        c0, w = COL_GROUPS[g]
        q2 = qf[:, c0:c0 + w, :].reshape(QR * w, LANES).astype(BF16)
        s = scores(q2, kops[slot][g], hh) + bm_ref[hh, t, group_rows(g), :]
        s_refs[slot][hh, group_rows(g), :] = s
        m_refs[slot][hh, group_rows(g), :] = jnp.broadcast_to(
            jnp.max(s, axis=-1, keepdims=True), (QR * w, LANES))

    def stage_softmax(slot, g, hh):
        rows = group_rows(g)
        m = jnp.concatenate([m_refs[slot][hh, rows, :]] * lane_tiles, axis=-1)
        p, l = softmax_terms(s_refs[slot][hh, rows, :], m)
        p_refs[slot][hh, rows, :] = p
        l_refs[slot][hh, rows, :] = jnp.broadcast_to(l, (rows.stop - rows.start, LANES))

    def stage_values(slot, g, hh):
        c0, w = COL_GROUPS[g]
        rows = group_rows(g)
        o = (jnp.dot(p_refs[slot][hh, rows, :], vops[slot][g], preferred_element_type=F32)
             / l_refs[slot][hh, rows, :])
        cols = slice(hh * DH, (hh + 1) * DH)
        oacc_ref[:, c0:c0 + w, cols] = o.reshape(QR, w, LANES)[:, :, cols]

    units = [(g, hh) for g in range(n_groups) for hh in range(2)]

    def step(u_values, softmax_slot, u_scores, slot):
        if u_values is not None:
            window_copy(vops[slot], v_ref, u_values)
        if u_scores is not None:
            window_copy(kops[slot], k_ref, u_scores)
            bb, rb = split(u_scores)
            qf = q_ref[bb, pl.ds(query_rows(rb), QB), :].astype(F32).reshape(QR, GW, LANES)
            t = block_type(rb)
        for g, hh in units:
            if u_scores is not None:
                stage_scores(t, slot, g, hh, qf)
            if softmax_slot is not None:
                stage_softmax(softmax_slot, g, hh)
            if u_values is not None:
                stage_values(slot, g, hh)
        if u_values is not None:
            bb, rb = split(u_values)
            o_ref[bb, pl.ds(query_rows(rb), QB), :] = (
                oacc_ref[...].reshape(QB, LANES).astype(BF16))

    n_u = ATT_BATCH * NRB
    step(None, None, 0, 0)
    step(None, 0, 1, 1)

    def pair(i, carry):
        u = 2 * i
        step(u - 2, 1, u, 0)
        step(u - 1, 0, u + 1, 1)
        return carry

    lax.fori_loop(1, n_u // 2, pair, 0)
    step(n_u - 2, 1, None, 0)
    step(n_u - 1, None, None, 1)

    for bb in range(ATT_BATCH):
        qm, km, vm = q_ref[bb, S:S + NM, :], k_ref[bb, S:S + NM, :], v_ref[bb, S:S + NM, :]
        outs = []
        for hh in range(2):
            sm = scores(qm, km, hh)
            p, l = softmax_terms(sm, jnp.max(sm, axis=-1, keepdims=True))
            outs.append(jnp.dot(p, vm, preferred_element_type=F32) / l)
        o = jnp.where(head_lanes(outs[0].shape)[0], outs[0], outs[1])
        o_ref[bb, S:S + NM, :] = o.astype(BF16)


def _attention(rpb_flat, q, k, v):
    blk = pl.BlockSpec((None, ATT_BATCH, LP, LANES), lambda hp, b: (hp, b, 0, 0))
    return pl.pallas_call(
        _attn_kernel,
        grid=(HEAD_PAIRS, B // ATT_BATCH),
        in_specs=[pl.BlockSpec(memory_space=pltpu.SMEM), blk, blk, blk],
        out_specs=blk,
        out_shape=jax.ShapeDtypeStruct((HEAD_PAIRS, B, LP, LANES), BF16),
        scratch_shapes=[
            pltpu.VMEM((2, 2 * WIN_H - 1, GW, LANES), F32),
            pltpu.VMEM((2, 3, QB, KSLOTS), F32),
            pltpu.VMEM((QR, GW, LANES), F32),
        ] + [pltpu.VMEM((len(COL_GROUPS), KSLOTS, LANES), BF16)] * 4
          + [pltpu.VMEM((2, QB, KSLOTS), F32)] * 2
          + [pltpu.VMEM((2, QB, KSLOTS), BF16)] * 2
          + [pltpu.VMEM((2, QB, LANES), F32)] * 4,
        compiler_params=pltpu.CompilerParams(
            dimension_semantics=("arbitrary", "arbitrary"), vmem_limit_bytes=VMEM_LIMIT),
        name="natten",
    )(rpb_flat, q, k, v)


def kernel(x, meta_tokens, norm_mix_g, norm_mlp_g, conv_w_in, conv_b_in, conv_w_dw, conv_b_dw, conv_ln_g, conv_ln_b, conv_w_out, conv_b_out, na_w_qkv, na_w_o, na_rpb, mlp_w1, mlp_w2, final_norm_g):
    assert x.shape == (B, S, D) and x.dtype == F32
    row = lambda a: a.reshape(1, -1).astype(F32)
    w_in, w_out = conv_w_in.astype(BF16), conv_w_out.astype(BF16)
    w_qkv, w_o = na_w_qkv.astype(BF16), na_w_o.astype(BF16)
    w1, w2 = mlp_w1.astype(BF16), mlp_w2.astype(BF16)
    meta = meta_tokens.astype(F32)
    h = None
    for i in range(DEPTH):
        j = i // 2
        last = i == DEPTH - 1
        if i % 2 == 0:
            g, b_in = row(norm_mix_g[i]), row(conv_b_in[j])
            if i == 0:
                u = _conv_in_first(x, meta, g, w_in, b_in, j)
                res, res_meta = x, meta
            else:
                u = _conv_in(h, g, w_in, b_in, j).reshape(B, LP * FEAT_TILES, LANES)
                res, res_meta = h.reshape(B, LP, D), None
            h = _conv_out(
                u, res, res_meta,
                conv_w_dw[j].reshape(CONV_W, SUBLANES, LANES),
                conv_b_dw[j].reshape(SUBLANES, LANES),
                row(conv_ln_g[j]), row(conv_ln_b[j]), w_out, row(conv_b_out[j]), j).reshape(NT, D)
            proj = None
        else:
            q, k, v = _qkv(h, row(norm_mix_g[i]), w_qkv, j)
            slabs = lambda a: a.reshape(HEAD_PAIRS, B, LP, LANES)
            o = _attention(na_rpb[j].reshape(-1), slabs(q), slabs(k), slabs(v))
            proj = (o if last else o.reshape(HEAD_PAIRS, NT, LANES), w_o, j)
        h = _mlp(h.reshape(B, LP, D) if last else h, row(norm_mlp_g[i]), w1, w2, i,
                 proj=proj, final_g=row(final_norm_g) if last else None)
    return h
```

```python
import functools

import jax
import jax.numpy as jnp
from jax import lax
from jax.experimental import pallas as pl
from jax.experimental.pallas import tpu as pltpu

D = 1024
B = 8
S = 4096
NM = 16
LP = S + NM
NT = B * LP
DEPTH = 4
GW = 64
ROWS = S // GW
NH = 16
DH = D // NH
WIN_H = 8
WIN_W = 16
CONV_W = 31
CONV_PAD = CONV_W // 2
DFF = 4 * D
EPS = 1e-6
NEG = -1e30

F32 = jnp.float32
BF16 = jnp.bfloat16

SUBLANES = 8
LANES = 128
VMEM_LIMIT = 56 * 1024 * 1024

TM = 512
HALO = 16
FEAT_TILES = D // LANES
CONV_ROWS = 16
CONV_UNROLL = 4
GLU_COLS = 256

QR = 8
NRB = ROWS // QR
QB = QR * GW
BAND_ROWS = QR + WIN_H - 1
COL_GROUPS = ((0, 24), (24, 16), (40, 24))
KEY_COL0 = (0, 16, 32)
KW = 32
GROUP_ROW0 = (0, 192, 320)
BAND = BAND_ROWS * KW
KSLOTS = 512
assert BAND + NM <= KSLOTS and KSLOTS % 256 == 0
assert all(max(c0 - WIN_W // 2, 0) >= k0 and min(c0 + w - 1 + WIN_W // 2, GW - 1) < k0 + KW
           for (c0, w), k0 in zip(COL_GROUPS, KEY_COL0))
LOG2E = 1.4426950408889634
ATT_BATCH = 2
HEAD_PAIRS = D // LANES


def _const_spec(shape):
    nd = len(shape)
    return pl.BlockSpec(shape, lambda *_: (0,) * nd, pipeline_mode=pl.Buffered(1))


def _layer_spec(shape, layer):
    nd = len(shape)
    return pl.BlockSpec((None,) + tuple(shape), lambda *_: (layer,) + (0,) * nd,
                        pipeline_mode=pl.Buffered(1))


def _rms(x, g):
    return x * lax.rsqrt(jnp.mean(x * x, axis=-1, keepdims=True) + EPS) * g


def _glu_rows(x, g_ref, w_ref, b_ref, u_ref):
    rows = x.shape[0]
    hn = _rms(x, g_ref[...]).astype(BF16)
    for c in range(D // GLU_COLS):
        lin = slice(c * GLU_COLS, (c + 1) * GLU_COLS)
        gate = slice(D + c * GLU_COLS, D + (c + 1) * GLU_COLS)
        a = jnp.dot(hn, w_ref[:, lin], preferred_element_type=F32) + b_ref[:, lin]
        gt = jnp.dot(hn, w_ref[:, gate], preferred_element_type=F32) + b_ref[:, gate]
        u = a * jax.nn.sigmoid(gt)
        for jj in range(GLU_COLS // LANES):
            j = c * (GLU_COLS // LANES) + jj
            u_ref[pl.ds(j, rows, stride=FEAT_TILES), :] = u[:, jj * LANES:(jj + 1) * LANES]


def _conv_in_kernel(x_ref, g_ref, w_ref, b_ref, u_ref):
    _glu_rows(x_ref[...], g_ref, w_ref, b_ref, u_ref)


def _conv_in_first_kernel(x_ref, meta_ref, g_ref, w_ref, b_ref, u_ref):
    i = pl.program_id(1)

    @pl.when(i < S // TM)
    def _grid_tokens():
        _glu_rows(x_ref[...], g_ref, w_ref, b_ref, u_ref)

    @pl.when(i == S // TM)
    def _meta_tokens():
        _glu_rows(meta_ref[...], g_ref, w_ref, b_ref, u_ref)


def _conv_in(hf, g, w_in, b_in, layer):
    return pl.pallas_call(
        _conv_in_kernel,
        grid=(pl.cdiv(NT, TM),),
        in_specs=[
            pl.BlockSpec((TM, D), lambda i: (i, 0)),
            _const_spec((1, D)),
            _layer_spec((D, 2 * D), layer),
            _const_spec((1, 2 * D)),
        ],
        out_specs=pl.BlockSpec((TM * FEAT_TILES, LANES), lambda i: (i, 0)),
        out_shape=jax.ShapeDtypeStruct((NT * FEAT_TILES, LANES), F32),
        compiler_params=pltpu.CompilerParams(
            dimension_semantics=("arbitrary",), vmem_limit_bytes=VMEM_LIMIT),
        name="conv_in",
    )(hf, g, w_in, b_in)


def _conv_in_first(x, meta_tokens, g, w_in, b_in, layer):
    n_main = S // TM
    return pl.pallas_call(
        _conv_in_first_kernel,
        grid=(B, n_main + 1),
        in_specs=[
            pl.BlockSpec((None, TM, D), lambda b, i: (b, jnp.minimum(i, n_main - 1), 0)),
            _const_spec((NM, D)),
            _const_spec((1, D)),
            _layer_spec((D, 2 * D), layer),
            _const_spec((1, 2 * D)),
        ],
        out_specs=pl.BlockSpec((None, TM * FEAT_TILES, LANES), lambda b, i: (b, i, 0)),
        out_shape=jax.ShapeDtypeStruct((B, LP * FEAT_TILES, LANES), F32),
        compiler_params=pltpu.CompilerParams(
            dimension_semantics=("arbitrary", "arbitrary"), vmem_limit_bytes=VMEM_LIMIT),
        name="conv_in_first",
    )(x, meta_tokens, g, w_in, b_in)


def _conv_out_kernel(left_ref, main_ref, right_ref, h_ref, hmeta_ref, wdw_ref, bdw_ref,
                     lng_ref, lnb_ref, wout_ref, bout_ref, o_ref, win_ref, y_ref):
    ft = FEAT_TILES
    n_main = S // TM
    i = pl.program_id(1)
    first = HALO - CONV_PAD

    def conv_rows(rows):
        bias = bdw_ref[...]

        def chunk(c, carry):
            t0 = c * CONV_ROWS
            accs = [bias] * CONV_ROWS
            for m in range(CONV_ROWS + CONV_W - 1):
                wv = win_ref[pl.ds(pl.multiple_of((t0 + first + m) * ft, ft), ft), :]
                for r in range(CONV_ROWS):
                    k = m - r
                    if 0 <= k < CONV_W:
                        accs[r] = accs[r] + wdw_ref[k] * wv
            for r in range(CONV_ROWS):
                y_ref[pl.ds(pl.multiple_of((t0 + r) * ft, ft), ft), :] = accs[r]
            return carry

        steps = rows // CONV_ROWS
        lax.fori_loop(0, steps, chunk, 0, unroll=min(CONV_UNROLL, steps))

    def finish(rows, res_ref):
        y = jnp.concatenate(
            [y_ref[pl.ds(j, rows, stride=ft), :] for j in range(ft)], axis=-1)
        mu = jnp.mean(y, axis=-1, keepdims=True)
        yc = y - mu
        yn = yc * lax.rsqrt(jnp.mean(yc * yc, axis=-1, keepdims=True) + EPS)
        yn = yn * lng_ref[...] + lnb_ref[...]
        act = (yn * jax.nn.sigmoid(yn)).astype(BF16)
        o_ref[0:rows, :] = (res_ref[0:rows, :] + bout_ref[...]
                            + jnp.dot(act, wout_ref[...], preferred_element_type=F32))

    zero_halo = jnp.zeros((HALO * ft, LANES), F32)

    @pl.when(i < n_main)
    def _grid_tokens():
        win_ref[pl.ds(0, HALO * ft), :] = left_ref[...]
        win_ref[pl.ds(HALO * ft, TM * ft), :] = main_ref[...]
        win_ref[pl.ds((HALO + TM) * ft, HALO * ft), :] = jnp.where(
            i < n_main - 1, right_ref[...], zero_halo)
        conv_rows(TM)
        finish(TM, h_ref)

    @pl.when(i == n_main)
    def _meta_tokens():
        win_ref[pl.ds(0, HALO * ft), :] = zero_halo
        win_ref[pl.ds(HALO * ft, NM * ft), :] = main_ref[pl.ds(0, NM * ft), :]
        win_ref[pl.ds((HALO + NM) * ft, HALO * ft), :] = right_ref[...]
        conv_rows(NM)
        finish(NM, hmeta_ref)


def _conv_out(u, res, res_meta, wdw, bdw, lng, lnb, w_out, b_out, layer):
    ft = FEAT_TILES
    n_main = S // TM
    halo_per_tile = TM // HALO
    meta_blk = S // HALO
    if res_meta is None:
        res_meta = res
        meta_spec = pl.BlockSpec((None, NM, D), lambda b, i: (b, S // NM, 0))
    else:
        meta_spec = _const_spec((NM, D))
    return pl.pallas_call(
        _conv_out_kernel,
        grid=(B, n_main + 1),
        in_specs=[
            pl.BlockSpec((None, HALO * ft, LANES),
                         lambda b, i: (b, jnp.where(i == 0, meta_blk, i * halo_per_tile - 1), 0)),
            pl.BlockSpec((None, TM * ft, LANES), lambda b, i: (b, i, 0)),
            pl.BlockSpec((None, HALO * ft, LANES),
                         lambda b, i: (b, jnp.where(i >= n_main - 1, 0, (i + 1) * halo_per_tile), 0)),
            pl.BlockSpec((None, TM, D), lambda b, i: (b, jnp.minimum(i, n_main - 1), 0)),
            meta_spec,
            _const_spec((CONV_W, SUBLANES, LANES)),
            _const_spec((SUBLANES, LANES)),
            _const_spec((1, D)),
            _const_spec((1, D)),
            _layer_spec((D, D), layer),
            _const_spec((1, D)),
        ],
        out_specs=pl.BlockSpec((None, TM, D), lambda b, i: (b, i, 0)),
        out_shape=jax.ShapeDtypeStruct((B, LP, D), F32),
        scratch_shapes=[pltpu.VMEM(((TM + 2 * HALO) * ft, LANES), F32),
                        pltpu.VMEM((TM * ft, LANES), F32)],
        compiler_params=pltpu.CompilerParams(
            dimension_semantics=("arbitrary", "arbitrary"), vmem_limit_bytes=VMEM_LIMIT),
        name="conv_out",
    )(u, u, u, res, res_meta, wdw, bdw, lng, lnb, w_out, b_out)


def _mlp_kernel(*refs, with_proj, with_final):
    refs = list(refs)
    h_ref = refs.pop(0)
    if with_proj:
        o_ref, wo_ref = refs.pop(0), refs.pop(0)
    g_ref, w1_ref, w2_ref = refs.pop(0), refs.pop(0), refs.pop(0)
    if with_final:
        gf_ref = refs.pop(0)
    out_ref, a_ref = refs

    h = h_ref[...]
    if with_proj:
        o = jnp.concatenate([o_ref[hp] for hp in range(HEAD_PAIRS)], axis=-1)
        h = h + jnp.dot(o, wo_ref[...], preferred_element_type=F32)
    hn = _rms(h, g_ref[...]).astype(BF16)
    for c in range(DFF // D):
        h1 = jnp.dot(hn, w1_ref[:, c * D:(c + 1) * D], preferred_element_type=F32)
        a_ref[:, c * D:(c + 1) * D] = jnp.square(jnp.maximum(h1, 0.0)).astype(BF16)
    out = h + jnp.dot(a_ref[...], w2_ref[...], preferred_element_type=F32)
    if with_final:
        out = _rms(out, gf_ref[...])
    out_ref[...] = out


def _mlp(h, g, w1, w2, layer, *, proj=None, final_g=None):
    with_proj = proj is not None
    with_final = final_g is not None
    if with_final:
        grid = (B, S // TM)
        tok = pl.BlockSpec((None, TM, D), lambda b, i: (b, i, 0))
        att = pl.BlockSpec((HEAD_PAIRS, None, TM, LANES), lambda b, i: (0, b, i, 0))
        out_shape = jax.ShapeDtypeStruct((B, S, D), F32)
        sem = ("arbitrary", "arbitrary")
    else:
        grid = (pl.cdiv(NT, TM),)
        tok = pl.BlockSpec((TM, D), lambda i: (i, 0))
        att = pl.BlockSpec((HEAD_PAIRS, TM, LANES), lambda i: (0, i, 0))
        out_shape = jax.ShapeDtypeStruct((NT, D), F32)
        sem = ("arbitrary",)
    args, specs = [h], [tok]
    if with_proj:
        o, w_o, attn_layer = proj
        args += [o, w_o]
        specs += [att, _layer_spec((D, D), attn_layer)]
    args += [g, w1, w2]
    specs += [_const_spec((1, D)), _layer_spec((D, DFF), layer), _layer_spec((DFF, D), layer)]
    if with_final:
        args.append(final_g)
        specs.append(_const_spec((1, D)))
    return pl.pallas_call(
        functools.partial(_mlp_kernel, with_proj=with_proj, with_final=with_final),
        grid=grid,
        in_specs=specs,
        out_specs=tok,
        out_shape=out_shape,
        scratch_shapes=[pltpu.VMEM((TM, DFF), BF16)],
        compiler_params=pltpu.CompilerParams(
            dimension_semantics=sem, vmem_limit_bytes=VMEM_LIMIT),
        name="mlp_final" if with_final else ("proj_mlp" if with_proj else "mlp"),
    )(*args)


def _qkv_kernel(x_ref, g_ref, w_ref, q_ref, k_ref, v_ref):
    hn = _rms(x_ref[...], g_ref[...]).astype(BF16)
    for idx, (o_ref, scale) in enumerate(((q_ref, DH ** -0.5 * LOG2E), (k_ref, None), (v_ref, None))):
        y = jnp.dot(hn, w_ref[:, idx * D:(idx + 1) * D], preferred_element_type=F32)
        if scale is not None:
            y = y * scale
        for hp in range(HEAD_PAIRS):
            o_ref[hp] = y[:, hp * LANES:(hp + 1) * LANES].astype(BF16)


def _qkv(hf, g, w_qkv, layer):
    tok = pl.BlockSpec((TM, D), lambda i: (i, 0))
    out = pl.BlockSpec((HEAD_PAIRS, TM, LANES), lambda i: (0, i, 0))
    return pl.pallas_call(
        _qkv_kernel,
        grid=(pl.cdiv(NT, TM),),
        in_specs=[tok, _const_spec((1, D)), _layer_spec((D, 3 * D), layer)],
        out_specs=[out, out, out],
        out_shape=[jax.ShapeDtypeStruct((HEAD_PAIRS, NT, LANES), BF16)] * 3,
        compiler_params=pltpu.CompilerParams(
            dimension_semantics=("arbitrary",), vmem_limit_bytes=VMEM_LIMIT),
        name="qkv",
    )(hf, g, w_qkv)


def _band_start(rb):
    return jnp.clip(rb * QR - WIN_H // 2, 0, ROWS - BAND_ROWS)


def _band_row_bias_index(t, i, br):
    rb = (0, 1, NRB - 1)[t]
    band0 = min(max(rb * QR - WIN_H // 2, 0), ROWS - BAND_ROWS)
    r = rb * QR + i
    sr = min(max(r - WIN_H // 2, 0), ROWS - WIN_H)
    kr = band0 + br
    return kr - r + WIN_H - 1 if sr <= kr < sr + WIN_H else None


def _attn_kernel(rpb_ref, q_ref, k_ref, v_ref, o_ref, tbl_ref, bm_ref, oacc_ref,
                 kop0, kop1, vop0, vop1, s0, s1, p0, p1, m0, m1, l0, l1):
    kops, vops = (kop0, kop1), (vop0, vop1)
    s_refs, p_refs, m_refs, l_refs = (s0, s1), (p0, p1), (m0, m1), (l0, l1)
    hp = pl.program_id(0)
    b = pl.program_id(1)
    n_groups = len(COL_GROUPS)
    lane_tiles = KSLOTS // LANES
    rows_per_tile = LANES // KW

    @pl.when(b == 0)
    def _build_bias():
        n_dr = 2 * WIN_H - 1
        n_dc = 2 * WIN_W - 1
        lane = lax.broadcasted_iota(jnp.int32, (GW, LANES), 1)
        qcol = lax.broadcasted_iota(jnp.int32, (GW, LANES), 0)
        key0 = jnp.where(qcol < COL_GROUPS[1][0], KEY_COL0[0],
                         jnp.where(qcol < COL_GROUPS[2][0], KEY_COL0[1], KEY_COL0[2]))
        kcol = key0 + lane % KW
        dcol = kcol - qcol + (WIN_W - 1)
        wstart = jnp.clip(qcol - WIN_W // 2, 0, GW - WIN_W)
        col_ok = (kcol >= wstart) & (kcol < wstart + WIN_W)
        for hh in range(2):
            head = hp * 2 + hh

            def build_table(dr, carry):
                base = (head * n_dr + dr) * n_dc
                tbl = jnp.full((GW, LANES), NEG, F32)
                for dc in range(n_dc):
                    tbl = jnp.where(dcol == dc, rpb_ref[base + dc] * LOG2E, tbl)
                tbl_ref[hh, dr] = jnp.where(col_ok, tbl, NEG)
                return carry

            lax.fori_loop(0, n_dr, build_table, 0)

            for t in range(3):
                for g, (c0, w) in enumerate(COL_GROUPS):
                    quarter = lax.broadcasted_iota(jnp.int32, (w, LANES), 1) // KW
                    within = lax.broadcasted_iota(jnp.int32, (w, LANES), 1) % KW
                    neg = jnp.full((w, LANES), NEG, F32)
                    for i in range(QR):
                        for lt in range(lane_tiles):
                            tile = neg
                            for qtr in range(rows_per_tile):
                                br = lt * rows_per_tile + qtr
                                if br < BAND_ROWS:
                                    dr = _band_row_bias_index(t, i, br)
                                    if dr is None:
                                        continue
                                    part = tbl_ref[hh, dr, c0:c0 + w, :]
                                else:
                                    part = jnp.where(within < NM, 0.0, NEG).astype(F32)
                                tile = jnp.where(quarter == qtr, part, tile)
                            r0 = GROUP_ROW0[g] + i * w
                            bm_ref[hh, t, r0:r0 + w, lt * LANES:(lt + 1) * LANES] = tile

    zpad = jnp.zeros((KSLOTS - BAND - NM, LANES), BF16)
    for kop_ref, vop_ref in zip(kops, vops):
        for g in range(n_groups):
            kop_ref[g, BAND + NM:, :] = zpad
            vop_ref[g, BAND + NM:, :] = zpad

    def head_lanes(shape):
        lane_q = lax.broadcasted_iota(jnp.int32, shape, 1)
        return [lane_q < DH, lane_q >= DH]

    def scores(q2, kop, hh):
        qh = jnp.where(head_lanes(q2.shape)[hh], q2, jnp.zeros_like(q2))
        return lax.dot_general(qh, kop, (((1,), (1,)), ((), ())),
                               preferred_element_type=F32)

    def softmax_terms(s, m):
        e = jnp.exp2(s - m)
        return e.astype(BF16), jnp.sum(e, axis=-1, keepdims=True)

    def split(u):
        if isinstance(u, int):
            return divmod(u, NRB)
        return u // NRB, u % NRB

    def window_copy(dst_ref, src_ref, u):
        bb, rb = split(u)
        band0 = _band_start(rb) * GW
        for g in range(n_groups):
            for br in range(BAND_ROWS):
                start = band0 + br * GW + KEY_COL0[g]
                if not isinstance(u, int):
                    start = pl.multiple_of(start, 16)
                dst_ref[g, br * KW:(br + 1) * KW, :] = src_ref[bb, pl.ds(start, KW), :]
            dst_ref[g, BAND:BAND + NM, :] = src_ref[bb, S:S + NM, :]

    def query_rows(rb):
        return rb * QB if isinstance(rb, int) else pl.multiple_of(rb * QB, QB)

    def block_type(rb):
        if isinstance(rb, int):
            return 0 if rb == 0 else (2 if rb == NRB - 1 else 1)
        return jnp.where(rb == 0, 0, jnp.where(rb == NRB - 1, 2, 1))

    def group_rows(g):
        return slice(GROUP_ROW0[g], GROUP_ROW0[g] + QR * COL_GROUPS[g][1])

    def stage_scores(t, slot, g, hh, qf):
        c0, w = COL_GROUPS[g]
        q2 = qf[:, c0:c0 + w, :].reshape(QR * w, LANES).astype(BF16)
        s = scores(q2, kops[slot][g], hh) + bm_ref[hh, t, group_rows(g), :]
        s_refs[slot][hh, group_rows(g), :] = s
        m_refs[slot][hh, group_rows(g), :] = jnp.broadcast_to(
            jnp.max(s, axis=-1, keepdims=True), (QR * w, LANES))

    def stage_softmax(slot, g, hh):
        rows = group_rows(g)
        m = jnp.concatenate([m_refs[slot][hh, rows, :]] * lane_tiles, axis=-1)
        p, l = softmax_terms(s_refs[slot][hh, rows, :], m)
        p_refs[slot][hh, rows, :] = p
        l_refs[slot][hh, rows, :] = jnp.broadcast_to(l, (rows.stop - rows.start, LANES))

    def stage_values(slot, g, hh):
        c0, w = COL_GROUPS[g]
        rows = group_rows(g)
        o = (jnp.dot(p_refs[slot][hh, rows, :], vops[slot][g], preferred_element_type=F32)
             / l_refs[slot][hh, rows, :])
        cols = slice(hh * DH, (hh + 1) * DH)
        oacc_ref[:, c0:c0 + w, cols] = o.reshape(QR, w, LANES)[:, :, cols]

    units = [(g, hh) for g in range(n_groups) for hh in range(2)]

    def step(u_values, softmax_slot, u_scores, slot):
        if u_values is not None:
            window_copy(vops[slot], v_ref, u_values)
        if u_scores is not None:
            window_copy(kops[slot], k_ref, u_scores)
            bb, rb = split(u_scores)
            qf = q_ref[bb, pl.ds(query_rows(rb), QB), :].astype(F32).reshape(QR, GW, LANES)
            t = block_type(rb)
        for g, hh in units:
            if u_scores is not None:
                stage_scores(t, slot, g, hh, qf)
            if softmax_slot is not None:
                stage_softmax(softmax_slot, g, hh)
            if u_values is not None:
                stage_values(slot, g, hh)
        if u_values is not None:
            bb, rb = split(u_values)
            o_ref[bb, pl.ds(query_rows(rb), QB), :] = (
                oacc_ref[...].reshape(QB, LANES).astype(BF16))

    n_u = ATT_BATCH * NRB
    step(None, None, 0, 0)
    step(None, 0, 1, 1)

    def pair(i, carry):
        u = 2 * i
        step(u - 2, 1, u, 0)
        step(u - 1, 0, u + 1, 1)
        return carry

    lax.fori_loop(1, n_u // 2, pair, 0)
    step(n_u - 2, 1, None, 0)
    step(n_u - 1, None, None, 1)

    for bb in range(ATT_BATCH):
        qm, km, vm = q_ref[bb, S:S + NM, :], k_ref[bb, S:S + NM, :], v_ref[bb, S:S + NM, :]
        outs = []
        for hh in range(2):
            sm = scores(qm, km, hh)
            p, l = softmax_terms(sm, jnp.max(sm, axis=-1, keepdims=True))
            outs.append(jnp.dot(p, vm, preferred_element_type=F32) / l)
        o = jnp.where(head_lanes(outs[0].shape)[0], outs[0], outs[1])
        o_ref[bb, S:S + NM, :] = o.astype(BF16)


def _attention(rpb_flat, q, k, v):
    blk = pl.BlockSpec((None, ATT_BATCH, LP, LANES), lambda hp, b: (hp, b, 0, 0))
    return pl.pallas_call(
        _attn_kernel,
        grid=(HEAD_PAIRS, B // ATT_BATCH),
        in_specs=[pl.BlockSpec(memory_space=pltpu.SMEM), blk, blk, blk],
        out_specs=blk,
        out_shape=jax.ShapeDtypeStruct((HEAD_PAIRS, B, LP, LANES), BF16),
        scratch_shapes=[
            pltpu.VMEM((2, 2 * WIN_H - 1, GW, LANES), F32),
            pltpu.VMEM((2, 3, QB, KSLOTS), F32),
            pltpu.VMEM((QR, GW, LANES), F32),
        ] + [pltpu.VMEM((len(COL_GROUPS), KSLOTS, LANES), BF16)] * 4
          + [pltpu.VMEM((2, QB, KSLOTS), F32)] * 2
          + [pltpu.VMEM((2, QB, KSLOTS), BF16)] * 2
          + [pltpu.VMEM((2, QB, LANES), F32)] * 4,
        compiler_params=pltpu.CompilerParams(
            dimension_semantics=("arbitrary", "arbitrary"), vmem_limit_bytes=VMEM_LIMIT),
        name="natten",
    )(rpb_flat, q, k, v)


def kernel(x, meta_tokens, norm_mix_g, norm_mlp_g, conv_w_in, conv_b_in, conv_w_dw, conv_b_dw, conv_ln_g, conv_ln_b, conv_w_out, conv_b_out, na_w_qkv, na_w_o, na_rpb, mlp_w1, mlp_w2, final_norm_g):
    assert x.shape == (B, S, D) and x.dtype == F32
    row = lambda a: a.reshape(1, -1).astype(F32)
    w_in, w_out = conv_w_in.astype(BF16), conv_w_out.astype(BF16)
    w_qkv, w_o = na_w_qkv.astype(BF16), na_w_o.astype(BF16)
    w1, w2 = mlp_w1.astype(BF16), mlp_w2.astype(BF16)
    meta = meta_tokens.astype(F32)
    h = None
    for i in range(DEPTH):
        j = i // 2
        last = i == DEPTH - 1
        if i % 2 == 0:
            g, b_in = row(norm_mix_g[i]), row(conv_b_in[j])
            if i == 0:
                u = _conv_in_first(x, meta, g, w_in, b_in, j)
                res, res_meta = x, meta
            else:
                u = _conv_in(h, g, w_in, b_in, j).reshape(B, LP * FEAT_TILES, LANES)
                res, res_meta = h.reshape(B, LP, D), None
            h = _conv_out(
                u, res, res_meta,
                conv_w_dw[j].reshape(CONV_W, SUBLANES, LANES),
                conv_b_dw[j].reshape(SUBLANES, LANES),
                row(conv_ln_g[j]), row(conv_ln_b[j]), w_out, row(conv_b_out[j]), j).reshape(NT, D)
            proj = None
        else:
            q, k, v = _qkv(h, row(norm_mix_g[i]), w_qkv, j)
            slabs = lambda a: a.reshape(HEAD_PAIRS, B, LP, LANES)
            o = _attention(na_rpb[j].reshape(-1), slabs(q), slabs(k), slabs(v))
            proj = (o if last else o.reshape(HEAD_PAIRS, NT, LANES), w_o, j)
        h = _mlp(h.reshape(B, LP, D) if last else h, row(norm_mlp_g[i]), w1, w2, i,
                 proj=proj, final_g=row(final_norm_g) if last else None)
    return h
```

```python
import functools

import jax
import jax.numpy as jnp
from jax import lax
from jax.experimental import pallas as pl
from jax.experimental.pallas import tpu as pltpu

D = 1024
B = 8
S = 4096
NM = 16
LP = S + NM
NT = B * LP
DEPTH = 4
GW = 64
ROWS = S // GW
NH = 16
DH = D // NH
WIN_H = 8
WIN_W = 16
CONV_W = 31
CONV_PAD = CONV_W // 2
DFF = 4 * D
EPS = 1e-6
NEG = -1e30

F32 = jnp.float32
BF16 = jnp.bfloat16

SUBLANES = 8
LANES = 128
VMEM_LIMIT = 56 * 1024 * 1024

TM = 512
TM_PROJ = 1024
HALO = 16
FEAT_TILES = D // LANES
CONV_ROWS = 16
CONV_UNROLL = 4
GLU_COLS = 256

QR = 8
NRB = ROWS // QR
QB = QR * GW
BAND_ROWS = QR + WIN_H - 1
COL_GROUPS = ((0, 24), (24, 16), (40, 24))
KEY_COL0 = (0, 16, 32)
KW = 32
GROUP_ROW0 = (0, 192, 320)
BAND = BAND_ROWS * KW
KSLOTS = 512
assert BAND + NM <= KSLOTS and KSLOTS % 256 == 0
assert all(max(c0 - WIN_W // 2, 0) >= k0 and min(c0 + w - 1 + WIN_W // 2, GW - 1) < k0 + KW
           for (c0, w), k0 in zip(COL_GROUPS, KEY_COL0))
LOG2E = 1.4426950408889634
ATT_BATCH = 2
HEAD_PAIRS = D // LANES


def _const_spec(shape):
    nd = len(shape)
    return pl.BlockSpec(shape, lambda *_: (0,) * nd, pipeline_mode=pl.Buffered(1))


def _layer_spec(shape, layer):
    nd = len(shape)
    return pl.BlockSpec((None,) + tuple(shape), lambda *_: (layer,) + (0,) * nd,
                        pipeline_mode=pl.Buffered(1))


def _rms(x, g):
    return x * lax.rsqrt(jnp.mean(x * x, axis=-1, keepdims=True) + EPS) * g


def _glu_rows(x, g_ref, w_ref, b_ref, u_ref):
    rows = x.shape[0]
    hn = _rms(x, g_ref[...]).astype(BF16)
    for c in range(D // GLU_COLS):
        lin = slice(c * GLU_COLS, (c + 1) * GLU_COLS)
        gate = slice(D + c * GLU_COLS, D + (c + 1) * GLU_COLS)
        a = jnp.dot(hn, w_ref[:, lin], preferred_element_type=F32) + b_ref[:, lin]
        gt = jnp.dot(hn, w_ref[:, gate], preferred_element_type=F32) + b_ref[:, gate]
        u = a * jax.nn.sigmoid(gt)
        for jj in range(GLU_COLS // LANES):
            j = c * (GLU_COLS // LANES) + jj
            u_ref[pl.ds(j, rows, stride=FEAT_TILES), :] = u[:, jj * LANES:(jj + 1) * LANES]


def _conv_in_kernel(x_ref, g_ref, w_ref, b_ref, u_ref):
    _glu_rows(x_ref[...], g_ref, w_ref, b_ref, u_ref)


def _conv_in_first_kernel(x_ref, meta_ref, g_ref, w_ref, b_ref, u_ref):
    i = pl.program_id(1)

    @pl.when(i < S // TM)
    def _grid_tokens():
        _glu_rows(x_ref[...], g_ref, w_ref, b_ref, u_ref)

    @pl.when(i == S // TM)
    def _meta_tokens():
        _glu_rows(meta_ref[...], g_ref, w_ref, b_ref, u_ref)


def _conv_in(hf, g, w_in, b_in, layer):
    return pl.pallas_call(
        _conv_in_kernel,
        grid=(pl.cdiv(NT, TM_PROJ),),
        in_specs=[
            pl.BlockSpec((TM_PROJ, D), lambda i: (i, 0)),
            _const_spec((1, D)),
            _layer_spec((D, 2 * D), layer),
            _const_spec((1, 2 * D)),
        ],
        out_specs=pl.BlockSpec((TM_PROJ * FEAT_TILES, LANES), lambda i: (i, 0)),
        out_shape=jax.ShapeDtypeStruct((NT * FEAT_TILES, LANES), F32),
        compiler_params=pltpu.CompilerParams(
            dimension_semantics=("arbitrary",), vmem_limit_bytes=VMEM_LIMIT),
        name="conv_in",
    )(hf, g, w_in, b_in)


def _conv_in_first(x, meta_tokens, g, w_in, b_in, layer):
    n_main = S // TM
    return pl.pallas_call(
        _conv_in_first_kernel,
        grid=(B, n_main + 1),
        in_specs=[
            pl.BlockSpec((None, TM, D), lambda b, i: (b, jnp.minimum(i, n_main - 1), 0)),
            _const_spec((NM, D)),
            _const_spec((1, D)),
            _layer_spec((D, 2 * D), layer),
            _const_spec((1, 2 * D)),
        ],
        out_specs=pl.BlockSpec((None, TM * FEAT_TILES, LANES), lambda b, i: (b, i, 0)),
        out_shape=jax.ShapeDtypeStruct((B, LP * FEAT_TILES, LANES), F32),
        compiler_params=pltpu.CompilerParams(
            dimension_semantics=("arbitrary", "arbitrary"), vmem_limit_bytes=VMEM_LIMIT),
        name="conv_in_first",
    )(x, meta_tokens, g, w_in, b_in)


def _conv_out_kernel(left_ref, main_ref, right_ref, h_ref, hmeta_ref, wdw_ref, bdw_ref,
                     lng_ref, lnb_ref, wout_ref, bout_ref, o_ref, win_ref, y_ref):
    ft = FEAT_TILES
    n_main = S // TM
    i = pl.program_id(1)
    first = HALO - CONV_PAD

    def conv_rows(rows):
        bias = bdw_ref[...]

        def chunk(c, carry):
            t0 = c * CONV_ROWS
            accs = [bias] * CONV_ROWS
            for m in range(CONV_ROWS + CONV_W - 1):
                wv = win_ref[pl.ds(pl.multiple_of((t0 + first + m) * ft, ft), ft), :]
                for r in range(CONV_ROWS):
                    k = m - r
                    if 0 <= k < CONV_W:
                        accs[r] = accs[r] + wdw_ref[k] * wv
            for r in range(CONV_ROWS):
                y_ref[pl.ds(pl.multiple_of((t0 + r) * ft, ft), ft), :] = accs[r]
            return carry

        steps = rows // CONV_ROWS
        lax.fori_loop(0, steps, chunk, 0, unroll=min(CONV_UNROLL, steps))

    def finish(rows, res_ref):
        y = jnp.concatenate(
            [y_ref[pl.ds(j, rows, stride=ft), :] for j in range(ft)], axis=-1)
        mu = jnp.mean(y, axis=-1, keepdims=True)
        yc = y - mu
        yn = yc * lax.rsqrt(jnp.mean(yc * yc, axis=-1, keepdims=True) + EPS)
        yn = yn * lng_ref[...] + lnb_ref[...]
        act = (yn * jax.nn.sigmoid(yn)).astype(BF16)
        o_ref[0:rows, :] = (res_ref[0:rows, :] + bout_ref[...]
                            + jnp.dot(act, wout_ref[...], preferred_element_type=F32))

    zero_halo = jnp.zeros((HALO * ft, LANES), F32)

    @pl.when(i < n_main)
    def _grid_tokens():
        win_ref[pl.ds(0, HALO * ft), :] = left_ref[...]
        win_ref[pl.ds(HALO * ft, TM * ft), :] = main_ref[...]
        win_ref[pl.ds((HALO + TM) * ft, HALO * ft), :] = jnp.where(
            i < n_main - 1, right_ref[...], zero_halo)
        conv_rows(TM)
        finish(TM, h_ref)

    @pl.when(i == n_main)
    def _meta_tokens():
        win_ref[pl.ds(0, HALO * ft), :] = zero_halo
        win_ref[pl.ds(HALO * ft, NM * ft), :] = main_ref[pl.ds(0, NM * ft), :]
        win_ref[pl.ds((HALO + NM) * ft, HALO * ft), :] = right_ref[...]
        conv_rows(NM)
        finish(NM, hmeta_ref)


def _conv_out(u, res, res_meta, wdw, bdw, lng, lnb, w_out, b_out, layer):
    ft = FEAT_TILES
    n_main = S // TM
    halo_per_tile = TM // HALO
    meta_blk = S // HALO
    if res_meta is None:
        res_meta = res
        meta_spec = pl.BlockSpec((None, NM, D), lambda b, i: (b, S // NM, 0))
    else:
        meta_spec = _const_spec((NM, D))
    return pl.pallas_call(
        _conv_out_kernel,
        grid=(B, n_main + 1),
        in_specs=[
            pl.BlockSpec((None, HALO * ft, LANES),
                         lambda b, i: (b, jnp.where(i == 0, meta_blk, i * halo_per_tile - 1), 0)),
            pl.BlockSpec((None, TM * ft, LANES), lambda b, i: (b, i, 0)),
            pl.BlockSpec((None, HALO * ft, LANES),
                         lambda b, i: (b, jnp.where(i >= n_main - 1, 0, (i + 1) * halo_per_tile), 0)),
            pl.BlockSpec((None, TM, D), lambda b, i: (b, jnp.minimum(i, n_main - 1), 0)),
            meta_spec,
            _const_spec((CONV_W, SUBLANES, LANES)),
            _const_spec((SUBLANES, LANES)),
            _const_spec((1, D)),
            _const_spec((1, D)),
            _layer_spec((D, D), layer),
            _const_spec((1, D)),
        ],
        out_specs=pl.BlockSpec((None, TM, D), lambda b, i: (b, i, 0)),
        out_shape=jax.ShapeDtypeStruct((B, LP, D), F32),
        scratch_shapes=[pltpu.VMEM(((TM + 2 * HALO) * ft, LANES), F32),
                        pltpu.VMEM((TM * ft, LANES), F32)],
        compiler_params=pltpu.CompilerParams(
            dimension_semantics=("arbitrary", "arbitrary"), vmem_limit_bytes=VMEM_LIMIT),
        name="conv_out",
    )(u, u, u, res, res_meta, wdw, bdw, lng, lnb, w_out, b_out)


def _mlp_kernel(*refs, with_proj, with_final):
    refs = list(refs)
    h_ref = refs.pop(0)
    if with_proj:
        o_ref, wo_ref = refs.pop(0), refs.pop(0)
    g_ref, w1_ref, w2_ref = refs.pop(0), refs.pop(0), refs.pop(0)
    if with_final:
        gf_ref = refs.pop(0)
    out_ref, a_ref = refs

    h = h_ref[...]
    if with_proj:
        o = jnp.concatenate([o_ref[hp] for hp in range(HEAD_PAIRS)], axis=-1)
        h = h + jnp.dot(o, wo_ref[...], preferred_element_type=F32)
    hn = _rms(h, g_ref[...]).astype(BF16)
    for c in range(DFF // D):
        h1 = jnp.dot(hn, w1_ref[:, c * D:(c + 1) * D], preferred_element_type=F32)
        a_ref[:, c * D:(c + 1) * D] = jnp.square(jnp.maximum(h1, 0.0)).astype(BF16)
    out = h + jnp.dot(a_ref[...], w2_ref[...], preferred_element_type=F32)
    if with_final:
        out = _rms(out, gf_ref[...])
    out_ref[...] = out


def _mlp(h, g, w1, w2, layer, *, proj=None, final_g=None):
    with_proj = proj is not None
    with_final = final_g is not None
    if with_final:
        grid = (B, S // TM)
        tok = pl.BlockSpec((None, TM, D), lambda b, i: (b, i, 0))
        att = pl.BlockSpec((HEAD_PAIRS, None, TM, LANES), lambda b, i: (0, b, i, 0))
        out_shape = jax.ShapeDtypeStruct((B, S, D), F32)
        sem = ("arbitrary", "arbitrary")
    else:
        grid = (pl.cdiv(NT, TM),)
        tok = pl.BlockSpec((TM, D), lambda i: (i, 0))
        att = pl.BlockSpec((HEAD_PAIRS, TM, LANES), lambda i: (0, i, 0))
        out_shape = jax.ShapeDtypeStruct((NT, D), F32)
        sem = ("arbitrary",)
    args, specs = [h], [tok]
    if with_proj:
        o, w_o, attn_layer = proj
        args += [o, w_o]
        specs += [att, _layer_spec((D, D), attn_layer)]
    args += [g, w1, w2]
    specs += [_const_spec((1, D)), _layer_spec((D, DFF), layer), _layer_spec((DFF, D), layer)]
    if with_final:
        args.append(final_g)
        specs.append(_const_spec((1, D)))
    return pl.pallas_call(
        functools.partial(_mlp_kernel, with_proj=with_proj, with_final=with_final),
        grid=grid,
        in_specs=specs,
        out_specs=tok,
        out_shape=out_shape,
        scratch_shapes=[pltpu.VMEM((TM, DFF), BF16)],
        compiler_params=pltpu.CompilerParams(
            dimension_semantics=sem, vmem_limit_bytes=VMEM_LIMIT),
        name="mlp_final" if with_final else ("proj_mlp" if with_proj else "mlp"),
    )(*args)


def _qkv_kernel(x_ref, g_ref, w_ref, q_ref, k_ref, v_ref):
    hn = _rms(x_ref[...], g_ref[...]).astype(BF16)
    for idx, (o_ref, scale) in enumerate(((q_ref, DH ** -0.5 * LOG2E), (k_ref, None), (v_ref, None))):
        y = jnp.dot(hn, w_ref[:, idx * D:(idx + 1) * D], preferred_element_type=F32)
        if scale is not None:
            y = y * scale
        for hp in range(HEAD_PAIRS):
            o_ref[hp] = y[:, hp * LANES:(hp + 1) * LANES].astype(BF16)


def _qkv(hf, g, w_qkv, layer):
    tok = pl.BlockSpec((TM_PROJ, D), lambda i: (i, 0))
    out = pl.BlockSpec((HEAD_PAIRS, TM_PROJ, LANES), lambda i: (0, i, 0))
    return pl.pallas_call(
        _qkv_kernel,
        grid=(pl.cdiv(NT, TM_PROJ),),
        in_specs=[tok, _const_spec((1, D)), _layer_spec((D, 3 * D), layer)],
        out_specs=[out, out, out],
        out_shape=[jax.ShapeDtypeStruct((HEAD_PAIRS, NT, LANES), BF16)] * 3,
        compiler_params=pltpu.CompilerParams(
            dimension_semantics=("arbitrary",), vmem_limit_bytes=VMEM_LIMIT),
        name="qkv",
    )(hf, g, w_qkv)


def _band_start(rb):
    return jnp.clip(rb * QR - WIN_H // 2, 0, ROWS - BAND_ROWS)


def _band_row_bias_index(t, i, br):
    rb = (0, 1, NRB - 1)[t]
    band0 = min(max(rb * QR - WIN_H // 2, 0), ROWS - BAND_ROWS)
    r = rb * QR + i
    sr = min(max(r - WIN_H // 2, 0), ROWS - WIN_H)
    kr = band0 + br
    return kr - r + WIN_H - 1 if sr <= kr < sr + WIN_H else None


def _attn_kernel(rpb_ref, q_ref, k_ref, v_ref, o_ref, tbl_ref, bm_ref, oacc_ref,
                 kop0, kop1, vop0, vop1, s0, s1, p0, p1, m0, m1, l0, l1):
    kops, vops = (kop0, kop1), (vop0, vop1)
    s_refs, p_refs, m_refs, l_refs = (s0, s1), (p0, p1), (m0, m1), (l0, l1)
    hp = pl.program_id(0)
    b = pl.program_id(1)
    n_groups = len(COL_GROUPS)
    lane_tiles = KSLOTS // LANES
    rows_per_tile = LANES // KW

    @pl.when(b == 0)
    def _build_bias():
        n_dr = 2 * WIN_H - 1
        n_dc = 2 * WIN_W - 1
        lane = lax.broadcasted_iota(jnp.int32, (GW, LANES), 1)
        qcol = lax.broadcasted_iota(jnp.int32, (GW, LANES), 0)
        key0 = jnp.where(qcol < COL_GROUPS[1][0], KEY_COL0[0],
                         jnp.where(qcol < COL_GROUPS[2][0], KEY_COL0[1], KEY_COL0[2]))
        kcol = key0 + lane % KW
        dcol = kcol - qcol + (WIN_W - 1)
        wstart = jnp.clip(qcol - WIN_W // 2, 0, GW - WIN_W)
        col_ok = (kcol >= wstart) & (kcol < wstart + WIN_W)
        for hh in range(2):
            head = hp * 2 + hh

            def build_table(dr, carry):
                base = (head * n_dr + dr) * n_dc
                tbl = jnp.full((GW, LANES), NEG, F32)
                for dc in range(n_dc):
                    tbl = jnp.where(dcol == dc, rpb_ref[base + dc] * LOG2E, tbl)
                tbl_ref[hh, dr] = jnp.where(col_ok, tbl, NEG)
                return carry

            lax.fori_loop(0, n_dr, build_table, 0)

            for t in range(3):
                for g, (c0, w) in enumerate(COL_GROUPS):
                    quarter = lax.broadcasted_iota(jnp.int32, (w, LANES), 1) // KW
                    within = lax.broadcasted_iota(jnp.int32, (w, LANES), 1) % KW
                    neg = jnp.full((w, LANES), NEG, F32)
                    for i in range(QR):
                        for lt in range(lane_tiles):
                            tile = neg
                            for qtr in range(rows_per_tile):
                                br = lt * rows_per_tile + qtr
                                if br < BAND_ROWS:
                                    dr = _band_row_bias_index(t, i, br)
                                    if dr is None:
                                        continue
                                    part = tbl_ref[hh, dr, c0:c0 + w, :]
                                else:
                                    part = jnp.where(within < NM, 0.0, NEG).astype(F32)
                                tile = jnp.where(quarter == qtr, part, tile)
                            r0 = GROUP_ROW0[g] + i * w
                            bm_ref[hh, t, r0:r0 + w, lt * LANES:(lt + 1) * LANES] = tile

    zpad = jnp.zeros((KSLOTS - BAND - NM, LANES), BF16)
    for kop_ref, vop_ref in zip(kops, vops):
        for g in range(n_groups):
            kop_ref[g, BAND + NM:, :] = zpad
            vop_ref[g, BAND + NM:, :] = zpad

    def head_lanes(shape):
        lane_q = lax.broadcasted_iota(jnp.int32, shape, 1)
        return [lane_q < DH, lane_q >= DH]

    def scores(q2, kop, hh):
        qh = jnp.where(head_lanes(q2.shape)[hh], q2, jnp.zeros_like(q2))
        return lax.dot_general(qh, kop, (((1,), (1,)), ((), ())),
                               preferred_element_type=F32)

    def softmax_terms(s, m):
        e = jnp.exp2(s - m)
        return e.astype(BF16), jnp.sum(e, axis=-1, keepdims=True)

    def split(u):
        if isinstance(u, int):
            return divmod(u, NRB)
        return u // NRB, u % NRB

    def window_copy(dst_ref, src_ref, u):
        bb, rb = split(u)
        band0 = _band_start(rb) * GW
        for g in range(n_groups):
            for br in range(BAND_ROWS):
                start = band0 + br * GW + KEY_COL0[g]
                if not isinstance(u, int):
                    start = pl.multiple_of(start, 16)
                dst_ref[g, br * KW:(br + 1) * KW, :] = src_ref[bb, pl.ds(start, KW), :]
            dst_ref[g, BAND:BAND + NM, :] = src_ref[bb, S:S + NM, :]

    def query_rows(rb):
        return rb * QB if isinstance(rb, int) else pl.multiple_of(rb * QB, QB)

    def block_type(rb):
        if isinstance(rb, int):
            return 0 if rb == 0 else (2 if rb == NRB - 1 else 1)
        return jnp.where(rb == 0, 0, jnp.where(rb == NRB - 1, 2, 1))

    def group_rows(g):
        return slice(GROUP_ROW0[g], GROUP_ROW0[g] + QR * COL_GROUPS[g][1])

    def stage_scores(t, slot, g, hh, qf):
        c0, w = COL_GROUPS[g]
        q2 = qf[:, c0:c0 + w, :].reshape(QR * w, LANES).astype(BF16)
        s = scores(q2, kops[slot][g], hh) + bm_ref[hh, t, group_rows(g), :]
        s_refs[slot][hh, group_rows(g), :] = s
        m_refs[slot][hh, group_rows(g), :] = jnp.broadcast_to(
            jnp.max(s, axis=-1, keepdims=True), (QR * w, LANES))

    def stage_softmax(slot, g, hh):
        rows = group_rows(g)
        m = jnp.concatenate([m_refs[slot][hh, rows, :]] * lane_tiles, axis=-1)
        p, l = softmax_terms(s_refs[slot][hh, rows, :], m)
        p_refs[slot][hh, rows, :] = p
        l_refs[slot][hh, rows, :] = jnp.broadcast_to(l, (rows.stop - rows.start, LANES))

    def stage_values(slot, g, hh):
        c0, w = COL_GROUPS[g]
        rows = group_rows(g)
        o = (jnp.dot(p_refs[slot][hh, rows, :], vops[slot][g], preferred_element_type=F32)
             / l_refs[slot][hh, rows, :])
        cols = slice(hh * DH, (hh + 1) * DH)
        oacc_ref[:, c0:c0 + w, cols] = o.reshape(QR, w, LANES)[:, :, cols]

    units = [(g, hh) for g in range(n_groups) for hh in range(2)]

    def step(u_values, softmax_slot, u_scores, slot):
        if u_values is not None:
            window_copy(vops[slot], v_ref, u_values)
        if u_scores is not None:
            window_copy(kops[slot], k_ref, u_scores)
            bb, rb = split(u_scores)
            qf = q_ref[bb, pl.ds(query_rows(rb), QB), :].astype(F32).reshape(QR, GW, LANES)
            t = block_type(rb)
        for g, hh in units:
            if u_scores is not None:
                stage_scores(t, slot, g, hh, qf)
            if softmax_slot is not None:
                stage_softmax(softmax_slot, g, hh)
            if u_values is not None:
                stage_values(slot, g, hh)
        if u_values is not None:
            bb, rb = split(u_values)
            o_ref[bb, pl.ds(query_rows(rb), QB), :] = (
                oacc_ref[...].reshape(QB, LANES).astype(BF16))

    n_u = ATT_BATCH * NRB
    step(None, None, 0, 0)
    step(None, 0, 1, 1)

    def pair(i, carry):
        u = 2 * i
        step(u - 2, 1, u, 0)
        step(u - 1, 0, u + 1, 1)
        return carry

    lax.fori_loop(1, n_u // 2, pair, 0)
    step(n_u - 2, 1, None, 0)
    step(n_u - 1, None, None, 1)

    for bb in range(ATT_BATCH):
        qm, km, vm = q_ref[bb, S:S + NM, :], k_ref[bb, S:S + NM, :], v_ref[bb, S:S + NM, :]
        outs = []
        for hh in range(2):
            sm = scores(qm, km, hh)
            p, l = softmax_terms(sm, jnp.max(sm, axis=-1, keepdims=True))
            outs.append(jnp.dot(p, vm, preferred_element_type=F32) / l)
        o = jnp.where(head_lanes(outs[0].shape)[0], outs[0], outs[1])
        o_ref[bb, S:S + NM, :] = o.astype(BF16)


def _attention(rpb_flat, q, k, v):
    blk = pl.BlockSpec((None, ATT_BATCH, LP, LANES), lambda hp, b: (hp, b, 0, 0))
    return pl.pallas_call(
        _attn_kernel,
        grid=(HEAD_PAIRS, B // ATT_BATCH),
        in_specs=[pl.BlockSpec(memory_space=pltpu.SMEM), blk, blk, blk],
        out_specs=blk,
        out_shape=jax.ShapeDtypeStruct((HEAD_PAIRS, B, LP, LANES), BF16),
        scratch_shapes=[
            pltpu.VMEM((2, 2 * WIN_H - 1, GW, LANES), F32),
            pltpu.VMEM((2, 3, QB, KSLOTS), F32),
            pltpu.VMEM((QR, GW, LANES), F32),
        ] + [pltpu.VMEM((len(COL_GROUPS), KSLOTS, LANES), BF16)] * 4
          + [pltpu.VMEM((2, QB, KSLOTS), F32)] * 2
          + [pltpu.VMEM((2, QB, KSLOTS), BF16)] * 2
          + [pltpu.VMEM((2, QB, LANES), F32)] * 4,
        compiler_params=pltpu.CompilerParams(
            dimension_semantics=("arbitrary", "arbitrary"), vmem_limit_bytes=VMEM_LIMIT),
        name="natten",
    )(rpb_flat, q, k, v)


def kernel(x, meta_tokens, norm_mix_g, norm_mlp_g, conv_w_in, conv_b_in, conv_w_dw, conv_b_dw, conv_ln_g, conv_ln_b, conv_w_out, conv_b_out, na_w_qkv, na_w_o, na_rpb, mlp_w1, mlp_w2, final_norm_g):
    assert x.shape == (B, S, D) and x.dtype == F32
    row = lambda a: a.reshape(1, -1).astype(F32)
    w_in, w_out = conv_w_in.astype(BF16), conv_w_out.astype(BF16)
    w_qkv, w_o = na_w_qkv.astype(BF16), na_w_o.astype(BF16)
    w1, w2 = mlp_w1.astype(BF16), mlp_w2.astype(BF16)
    meta = meta_tokens.astype(F32)
    h = None
    for i in range(DEPTH):
        j = i // 2
        last = i == DEPTH - 1
        if i % 2 == 0:
            g, b_in = row(norm_mix_g[i]), row(conv_b_in[j])
            if i == 0:
                u = _conv_in_first(x, meta, g, w_in, b_in, j)
                res, res_meta = x, meta
            else:
                u = _conv_in(h, g, w_in, b_in, j).reshape(B, LP * FEAT_TILES, LANES)
                res, res_meta = h.reshape(B, LP, D), None
            h = _conv_out(
                u, res, res_meta,
                conv_w_dw[j].reshape(CONV_W, SUBLANES, LANES),
                conv_b_dw[j].reshape(SUBLANES, LANES),
                row(conv_ln_g[j]), row(conv_ln_b[j]), w_out, row(conv_b_out[j]), j).reshape(NT, D)
            proj = None
        else:
            q, k, v = _qkv(h, row(norm_mix_g[i]), w_qkv, j)
            slabs = lambda a: a.reshape(HEAD_PAIRS, B, LP, LANES)
            o = _attention(na_rpb[j].reshape(-1), slabs(q), slabs(k), slabs(v))
            proj = (o if last else o.reshape(HEAD_PAIRS, NT, LANES), w_o, j)
        h = _mlp(h.reshape(B, LP, D) if last else h, row(norm_mlp_g[i]), w1, w2, i,
                 proj=proj, final_g=row(final_norm_g) if last else None)
    return h
```

```python
import functools

import jax
import jax.numpy as jnp
from jax import lax
from jax.experimental import pallas as pl
from jax.experimental.pallas import tpu as pltpu

D = 1024
B = 8
S = 4096
NM = 16
LP = S + NM
NT = B * LP
DEPTH = 4
GW = 64
ROWS = S // GW
NH = 16
DH = D // NH
WIN_H = 8
WIN_W = 16
CONV_W = 31
CONV_PAD = CONV_W // 2
DFF = 4 * D
EPS = 1e-6
NEG = -1e30

F32 = jnp.float32
BF16 = jnp.bfloat16

SUBLANES = 8
LANES = 128
VMEM_LIMIT = 56 * 1024 * 1024

TM = 512
TM_PROJ = 1024
HALO = 16
FEAT_TILES = D // LANES
CONV_ROWS = 16
CONV_UNROLL = 8
GLU_COLS = 256

QR = 8
NRB = ROWS // QR
QB = QR * GW
BAND_ROWS = QR + WIN_H - 1
COL_GROUPS = ((0, 24), (24, 16), (40, 24))
KEY_COL0 = (0, 16, 32)
KW = 32
GROUP_ROW0 = (0, 192, 320)
BAND = BAND_ROWS * KW
KSLOTS = 512
assert BAND + NM <= KSLOTS and KSLOTS % 256 == 0
assert all(max(c0 - WIN_W // 2, 0) >= k0 and min(c0 + w - 1 + WIN_W // 2, GW - 1) < k0 + KW
           for (c0, w), k0 in zip(COL_GROUPS, KEY_COL0))
LOG2E = 1.4426950408889634
ATT_BATCH = 2
HEAD_PAIRS = D // LANES


def _const_spec(shape):
    nd = len(shape)
    return pl.BlockSpec(shape, lambda *_: (0,) * nd, pipeline_mode=pl.Buffered(1))


def _layer_spec(shape, layer):
    nd = len(shape)
    return pl.BlockSpec((None,) + tuple(shape), lambda *_: (layer,) + (0,) * nd,
                        pipeline_mode=pl.Buffered(1))


def _rms(x, g):
    return x * lax.rsqrt(jnp.mean(x * x, axis=-1, keepdims=True) + EPS) * g


def _glu_rows(x, g_ref, w_ref, b_ref, u_ref):
    rows = x.shape[0]
    hn = _rms(x, g_ref[...]).astype(BF16)
    for c in range(D // GLU_COLS):
        lin = slice(c * GLU_COLS, (c + 1) * GLU_COLS)
        gate = slice(D + c * GLU_COLS, D + (c + 1) * GLU_COLS)
        a = jnp.dot(hn, w_ref[:, lin], preferred_element_type=F32) + b_ref[:, lin]
        gt = jnp.dot(hn, w_ref[:, gate], preferred_element_type=F32) + b_ref[:, gate]
        u = a * jax.nn.sigmoid(gt)
        for jj in range(GLU_COLS // LANES):
            j = c * (GLU_COLS // LANES) + jj
            u_ref[pl.ds(j, rows, stride=FEAT_TILES), :] = u[:, jj * LANES:(jj + 1) * LANES]


def _conv_in_kernel(x_ref, g_ref, w_ref, b_ref, u_ref):
    _glu_rows(x_ref[...], g_ref, w_ref, b_ref, u_ref)


def _conv_in_first_kernel(x_ref, meta_ref, g_ref, w_ref, b_ref, u_ref):
    i = pl.program_id(1)

    @pl.when(i < S // TM_PROJ)
    def _grid_tokens():
        _glu_rows(x_ref[...], g_ref, w_ref, b_ref, u_ref)

    @pl.when(i == S // TM_PROJ)
    def _meta_tokens():
        _glu_rows(meta_ref[...], g_ref, w_ref, b_ref, u_ref)


def _conv_in(hf, g, w_in, b_in, layer):
    return pl.pallas_call(
        _conv_in_kernel,
        grid=(pl.cdiv(NT, TM_PROJ),),
        in_specs=[
            pl.BlockSpec((TM_PROJ, D), lambda i: (i, 0)),
            _const_spec((1, D)),
            _layer_spec((D, 2 * D), layer),
            _const_spec((1, 2 * D)),
        ],
        out_specs=pl.BlockSpec((TM_PROJ * FEAT_TILES, LANES), lambda i: (i, 0)),
        out_shape=jax.ShapeDtypeStruct((NT * FEAT_TILES, LANES), F32),
        compiler_params=pltpu.CompilerParams(
            dimension_semantics=("arbitrary",), vmem_limit_bytes=VMEM_LIMIT),
        name="conv_in",
    )(hf, g, w_in, b_in)


def _conv_in_first(x, meta_tokens, g, w_in, b_in, layer):
    n_main = S // TM_PROJ
    return pl.pallas_call(
        _conv_in_first_kernel,
        grid=(B, n_main + 1),
        in_specs=[
            pl.BlockSpec((None, TM_PROJ, D), lambda b, i: (b, jnp.minimum(i, n_main - 1), 0)),
            _const_spec((NM, D)),
            _const_spec((1, D)),
            _layer_spec((D, 2 * D), layer),
            _const_spec((1, 2 * D)),
        ],
        out_specs=pl.BlockSpec((None, TM_PROJ * FEAT_TILES, LANES), lambda b, i: (b, i, 0)),
        out_shape=jax.ShapeDtypeStruct((B, LP * FEAT_TILES, LANES), F32),
        compiler_params=pltpu.CompilerParams(
            dimension_semantics=("arbitrary", "arbitrary"), vmem_limit_bytes=VMEM_LIMIT),
        name="conv_in_first",
    )(x, meta_tokens, g, w_in, b_in)


def _conv_out_kernel(left_ref, main_ref, right_ref, h_ref, hmeta_ref, wdw_ref, bdw_ref,
                     lng_ref, lnb_ref, wout_ref, bout_ref, o_ref, win_ref, y_ref):
    ft = FEAT_TILES
    n_main = S // TM
    i = pl.program_id(1)
    first = HALO - CONV_PAD

    def conv_rows(rows):
        bias = bdw_ref[...]

        def chunk(c, carry):
            t0 = c * CONV_ROWS
            accs = [bias] * CONV_ROWS
            for m in range(CONV_ROWS + CONV_W - 1):
                wv = win_ref[pl.ds(pl.multiple_of((t0 + first + m) * ft, ft), ft), :]
                for r in range(CONV_ROWS):
                    k = m - r
                    if 0 <= k < CONV_W:
                        accs[r] = accs[r] + wdw_ref[k] * wv
            for r in range(CONV_ROWS):
                y_ref[pl.ds(pl.multiple_of((t0 + r) * ft, ft), ft), :] = accs[r]
            return carry

        steps = rows // CONV_ROWS
        lax.fori_loop(0, steps, chunk, 0, unroll=min(CONV_UNROLL, steps))

    def finish(rows, res_ref):
        y = jnp.concatenate(
            [y_ref[pl.ds(j, rows, stride=ft), :] for j in range(ft)], axis=-1)
        mu = jnp.mean(y, axis=-1, keepdims=True)
        yc = y - mu
        yn = yc * lax.rsqrt(jnp.mean(yc * yc, axis=-1, keepdims=True) + EPS)
        yn = yn * lng_ref[...] + lnb_ref[...]
        act = (yn * jax.nn.sigmoid(yn)).astype(BF16)
        o_ref[0:rows, :] = (res_ref[0:rows, :] + bout_ref[...]
                            + jnp.dot(act, wout_ref[...], preferred_element_type=F32))

    zero_halo = jnp.zeros((HALO * ft, LANES), F32)

    @pl.when(i < n_main)
    def _grid_tokens():
        win_ref[pl.ds(0, HALO * ft), :] = left_ref[...]
        win_ref[pl.ds(HALO * ft, TM * ft), :] = main_ref[...]
        win_ref[pl.ds((HALO + TM) * ft, HALO * ft), :] = jnp.where(
            i < n_main - 1, right_ref[...], zero_halo)
        conv_rows(TM)
        finish(TM, h_ref)

    @pl.when(i == n_main)
    def _meta_tokens():
        win_ref[pl.ds(0, HALO * ft), :] = zero_halo
        win_ref[pl.ds(HALO * ft, NM * ft), :] = main_ref[pl.ds(0, NM * ft), :]
        win_ref[pl.ds((HALO + NM) * ft, HALO * ft), :] = right_ref[...]
        conv_rows(NM)
        finish(NM, hmeta_ref)


def _conv_out(u, res, res_meta, wdw, bdw, lng, lnb, w_out, b_out, layer):
    ft = FEAT_TILES
    n_main = S // TM
    halo_per_tile = TM // HALO
    meta_blk = S // HALO
    if res_meta is None:
        res_meta = res
        meta_spec = pl.BlockSpec((None, NM, D), lambda b, i: (b, S // NM, 0))
    else:
        meta_spec = _const_spec((NM, D))
    return pl.pallas_call(
        _conv_out_kernel,
        grid=(B, n_main + 1),
        in_specs=[
            pl.BlockSpec((None, HALO * ft, LANES),
                         lambda b, i: (b, jnp.where(i == 0, meta_blk, i * halo_per_tile - 1), 0)),
            pl.BlockSpec((None, TM * ft, LANES), lambda b, i: (b, i, 0)),
            pl.BlockSpec((None, HALO * ft, LANES),
                         lambda b, i: (b, jnp.where(i >= n_main - 1, 0, (i + 1) * halo_per_tile), 0)),
            pl.BlockSpec((None, TM, D), lambda b, i: (b, jnp.minimum(i, n_main - 1), 0)),
            meta_spec,
            _const_spec((CONV_W, SUBLANES, LANES)),
            _const_spec((SUBLANES, LANES)),
            _const_spec((1, D)),
            _const_spec((1, D)),
            _layer_spec((D, D), layer),
            _const_spec((1, D)),
        ],
        out_specs=pl.BlockSpec((None, TM, D), lambda b, i: (b, i, 0)),
        out_shape=jax.ShapeDtypeStruct((B, LP, D), F32),
        scratch_shapes=[pltpu.VMEM(((TM + 2 * HALO) * ft, LANES), F32),
                        pltpu.VMEM((TM * ft, LANES), F32)],
        compiler_params=pltpu.CompilerParams(
            dimension_semantics=("arbitrary", "arbitrary"), vmem_limit_bytes=VMEM_LIMIT),
        name="conv_out",
    )(u, u, u, res, res_meta, wdw, bdw, lng, lnb, w_out, b_out)


def _mlp_kernel(*refs, with_proj, with_final):
    refs = list(refs)
    h_ref = refs.pop(0)
    if with_proj:
        o_ref, wo_ref = refs.pop(0), refs.pop(0)
    g_ref, w1_ref, w2_ref = refs.pop(0), refs.pop(0), refs.pop(0)
    if with_final:
        gf_ref = refs.pop(0)
    out_ref, a_ref = refs

    h = h_ref[...]
    if with_proj:
        o = jnp.concatenate([o_ref[hp] for hp in range(HEAD_PAIRS)], axis=-1)
        h = h + jnp.dot(o, wo_ref[...], preferred_element_type=F32)
    hn = _rms(h, g_ref[...]).astype(BF16)
    for c in range(DFF // D):
        h1 = jnp.dot(hn, w1_ref[:, c * D:(c + 1) * D], preferred_element_type=F32)
        a_ref[:, c * D:(c + 1) * D] = jnp.square(jnp.maximum(h1, 0.0)).astype(BF16)
    out = h + jnp.dot(a_ref[...], w2_ref[...], preferred_element_type=F32)
    if with_final:
        out = _rms(out, gf_ref[...])
    out_ref[...] = out


def _mlp(h, g, w1, w2, layer, *, proj=None, final_g=None):
    with_proj = proj is not None
    with_final = final_g is not None
    if with_final:
        grid = (B, S // TM)
        tok = pl.BlockSpec((None, TM, D), lambda b, i: (b, i, 0))
        att = pl.BlockSpec((HEAD_PAIRS, None, TM, LANES), lambda b, i: (0, b, i, 0))
        out_shape = jax.ShapeDtypeStruct((B, S, D), F32)
        sem = ("arbitrary", "arbitrary")
    else:
        grid = (pl.cdiv(NT, TM),)
        tok = pl.BlockSpec((TM, D), lambda i: (i, 0))
        att = pl.BlockSpec((HEAD_PAIRS, TM, LANES), lambda i: (0, i, 0))
        out_shape = jax.ShapeDtypeStruct((NT, D), F32)
        sem = ("arbitrary",)
    args, specs = [h], [tok]
    if with_proj:
        o, w_o, attn_layer = proj
        args += [o, w_o]
        specs += [att, _layer_spec((D, D), attn_layer)]
    args += [g, w1, w2]
    specs += [_const_spec((1, D)), _layer_spec((D, DFF), layer), _layer_spec((DFF, D), layer)]
    if with_final:
        args.append(final_g)
        specs.append(_const_spec((1, D)))
    return pl.pallas_call(
        functools.partial(_mlp_kernel, with_proj=with_proj, with_final=with_final),
        grid=grid,
        in_specs=specs,
        out_specs=tok,
        out_shape=out_shape,
        scratch_shapes=[pltpu.VMEM((TM, DFF), BF16)],
        compiler_params=pltpu.CompilerParams(
            dimension_semantics=sem, vmem_limit_bytes=VMEM_LIMIT),
        name="mlp_final" if with_final else ("proj_mlp" if with_proj else "mlp"),
    )(*args)


def _qkv_kernel(x_ref, g_ref, w_ref, q_ref, k_ref, v_ref):
    hn = _rms(x_ref[...], g_ref[...]).astype(BF16)
    for idx, (o_ref, scale) in enumerate(((q_ref, DH ** -0.5 * LOG2E), (k_ref, None), (v_ref, None))):
        y = jnp.dot(hn, w_ref[:, idx * D:(idx + 1) * D], preferred_element_type=F32)
        if scale is not None:
            y = y * scale
        for hp in range(HEAD_PAIRS):
            o_ref[hp] = y[:, hp * LANES:(hp + 1) * LANES].astype(BF16)


def _qkv(hf, g, w_qkv, layer):
    tok = pl.BlockSpec((TM_PROJ, D), lambda i: (i, 0))
    out = pl.BlockSpec((HEAD_PAIRS, TM_PROJ, LANES), lambda i: (0, i, 0))
    return pl.pallas_call(
        _qkv_kernel,
        grid=(pl.cdiv(NT, TM_PROJ),),
        in_specs=[tok, _const_spec((1, D)), _layer_spec((D, 3 * D), layer)],
        out_specs=[out, out, out],
        out_shape=[jax.ShapeDtypeStruct((HEAD_PAIRS, NT, LANES), BF16)] * 3,
        compiler_params=pltpu.CompilerParams(
            dimension_semantics=("arbitrary",), vmem_limit_bytes=VMEM_LIMIT),
        name="qkv",
    )(hf, g, w_qkv)


def _band_start(rb):
    return jnp.clip(rb * QR - WIN_H // 2, 0, ROWS - BAND_ROWS)


def _band_row_bias_index(t, i, br):
    rb = (0, 1, NRB - 1)[t]
    band0 = min(max(rb * QR - WIN_H // 2, 0), ROWS - BAND_ROWS)
    r = rb * QR + i
    sr = min(max(r - WIN_H // 2, 0), ROWS - WIN_H)
    kr = band0 + br
    return kr - r + WIN_H - 1 if sr <= kr < sr + WIN_H else None


def _attn_kernel(rpb_ref, q_ref, k_ref, v_ref, o_ref, tbl_ref, bm_ref, oacc_ref,
                 kop0, kop1, vop0, vop1, s0, s1, p0, p1, m0, m1, l0, l1):
    kops, vops = (kop0, kop1), (vop0, vop1)
    s_refs, p_refs, m_refs, l_refs = (s0, s1), (p0, p1), (m0, m1), (l0, l1)
    hp = pl.program_id(0)
    b = pl.program_id(1)
    n_groups = len(COL_GROUPS)
    lane_tiles = KSLOTS // LANES
    rows_per_tile = LANES // KW

    @pl.when(b == 0)
    def _build_bias():
        n_dr = 2 * WIN_H - 1
        n_dc = 2 * WIN_W - 1
        lane = lax.broadcasted_iota(jnp.int32, (GW, LANES), 1)
        qcol = lax.broadcasted_iota(jnp.int32, (GW, LANES), 0)
        key0 = jnp.where(qcol < COL_GROUPS[1][0], KEY_COL0[0],
                         jnp.where(qcol < COL_GROUPS[2][0], KEY_COL0[1], KEY_COL0[2]))
        kcol = key0 + lane % KW
        dcol = kcol - qcol + (WIN_W - 1)
        wstart = jnp.clip(qcol - WIN_W // 2, 0, GW - WIN_W)
        col_ok = (kcol >= wstart) & (kcol < wstart + WIN_W)
        for hh in range(2):
            head = hp * 2 + hh

            def build_table(dr, carry):
                base = (head * n_dr + dr) * n_dc
                tbl = jnp.full((GW, LANES), NEG, F32)
                for dc in range(n_dc):
                    tbl = jnp.where(dcol == dc, rpb_ref[base + dc] * LOG2E, tbl)
                tbl_ref[hh, dr] = jnp.where(col_ok, tbl, NEG)
                return carry

            lax.fori_loop(0, n_dr, build_table, 0)

            for t in range(3):
                for g, (c0, w) in enumerate(COL_GROUPS):
                    quarter = lax.broadcasted_iota(jnp.int32, (w, LANES), 1) // KW
                    within = lax.broadcasted_iota(jnp.int32, (w, LANES), 1) % KW
                    neg = jnp.full((w, LANES), NEG, F32)
                    for i in range(QR):
                        for lt in range(lane_tiles):
                            tile = neg
                            for qtr in range(rows_per_tile):
                                br = lt * rows_per_tile + qtr
                                if br < BAND_ROWS:
                                    dr = _band_row_bias_index(t, i, br)
                                    if dr is None:
                                        continue
                                    part = tbl_ref[hh, dr, c0:c0 + w, :]
                                else:
                                    part = jnp.where(within < NM, 0.0, NEG).astype(F32)
                                tile = jnp.where(quarter == qtr, part, tile)
                            r0 = GROUP_ROW0[g] + i * w
                            bm_ref[hh, t, r0:r0 + w, lt * LANES:(lt + 1) * LANES] = tile

    zpad = jnp.zeros((KSLOTS - BAND - NM, LANES), BF16)
    for kop_ref, vop_ref in zip(kops, vops):
        for g in range(n_groups):
            kop_ref[g, BAND + NM:, :] = zpad
            vop_ref[g, BAND + NM:, :] = zpad

    def head_lanes(shape):
        lane_q = lax.broadcasted_iota(jnp.int32, shape, 1)
        return [lane_q < DH, lane_q >= DH]

    def scores(q2, kop, hh):
        qh = jnp.where(head_lanes(q2.shape)[hh], q2, jnp.zeros_like(q2))
        return lax.dot_general(qh, kop, (((1,), (1,)), ((), ())),
                               preferred_element_type=F32)

    def softmax_terms(s, m):
        e = jnp.exp2(s - m)
        return e.astype(BF16), jnp.sum(e, axis=-1, keepdims=True)

    def split(u):
        if isinstance(u, int):
            return divmod(u, NRB)
        return u // NRB, u % NRB

    def window_copy(dst_ref, src_ref, u):
        bb, rb = split(u)
        band0 = _band_start(rb) * GW
        for g in range(n_groups):
            for br in range(BAND_ROWS):
                start = band0 + br * GW + KEY_COL0[g]
                if not isinstance(u, int):
                    start = pl.multiple_of(start, 16)
                dst_ref[g, br * KW:(br + 1) * KW, :] = src_ref[bb, pl.ds(start, KW), :]
            dst_ref[g, BAND:BAND + NM, :] = src_ref[bb, S:S + NM, :]

    def query_rows(rb):
        return rb * QB if isinstance(rb, int) else pl.multiple_of(rb * QB, QB)

    def block_type(rb):
        if isinstance(rb, int):
            return 0 if rb == 0 else (2 if rb == NRB - 1 else 1)
        return jnp.where(rb == 0, 0, jnp.where(rb == NRB - 1, 2, 1))

    def group_rows(g):
        return slice(GROUP_ROW0[g], GROUP_ROW0[g] + QR * COL_GROUPS[g][1])

    def stage_scores(t, slot, g, hh, qf):
        c0, w = COL_GROUPS[g]
        q2 = qf[:, c0:c0 + w, :].reshape(QR * w, LANES).astype(BF16)
        s = scores(q2, kops[slot][g], hh) + bm_ref[hh, t, group_rows(g), :]
        s_refs[slot][hh, group_rows(g), :] = s
        m_refs[slot][hh, group_rows(g), :] = jnp.broadcast_to(
            jnp.max(s, axis=-1, keepdims=True), (QR * w, LANES))

    def stage_softmax(slot, g, hh):
        rows = group_rows(g)
        m = jnp.concatenate([m_refs[slot][hh, rows, :]] * lane_tiles, axis=-1)
        p, l = softmax_terms(s_refs[slot][hh, rows, :], m)
        p_refs[slot][hh, rows, :] = p
        l_refs[slot][hh, rows, :] = jnp.broadcast_to(l, (rows.stop - rows.start, LANES))

    def stage_values(slot, g, hh):
        c0, w = COL_GROUPS[g]
        rows = group_rows(g)
        o = (jnp.dot(p_refs[slot][hh, rows, :], vops[slot][g], preferred_element_type=F32)
             / l_refs[slot][hh, rows, :])
        cols = slice(hh * DH, (hh + 1) * DH)
        oacc_ref[:, c0:c0 + w, cols] = o.reshape(QR, w, LANES)[:, :, cols]

    units = [(g, hh) for g in range(n_groups) for hh in range(2)]

    def step(u_values, softmax_slot, u_scores, slot):
        if u_values is not None:
            window_copy(vops[slot], v_ref, u_values)
        if u_scores is not None:
            window_copy(kops[slot], k_ref, u_scores)
            bb, rb = split(u_scores)
            qf = q_ref[bb, pl.ds(query_rows(rb), QB), :].astype(F32).reshape(QR, GW, LANES)
            t = block_type(rb)
        for g, hh in units:
            if u_scores is not None:
                stage_scores(t, slot, g, hh, qf)
            if softmax_slot is not None:
                stage_softmax(softmax_slot, g, hh)
            if u_values is not None:
                stage_values(slot, g, hh)
        if u_values is not None:
            bb, rb = split(u_values)
            o_ref[bb, pl.ds(query_rows(rb), QB), :] = (
                oacc_ref[...].reshape(QB, LANES).astype(BF16))

    n_u = ATT_BATCH * NRB
    step(None, None, 0, 0)
    step(None, 0, 1, 1)

    def pair(i, carry):
        u = 2 * i
        step(u - 2, 1, u, 0)
        step(u - 1, 0, u + 1, 1)
        return carry

    lax.fori_loop(1, n_u // 2, pair, 0)
    step(n_u - 2, 1, None, 0)
    step(n_u - 1, None, None, 1)

    for bb in range(ATT_BATCH):
        qm, km, vm = q_ref[bb, S:S + NM, :], k_ref[bb, S:S + NM, :], v_ref[bb, S:S + NM, :]
        outs = []
        for hh in range(2):
            sm = scores(qm, km, hh)
            p, l = softmax_terms(sm, jnp.max(sm, axis=-1, keepdims=True))
            outs.append(jnp.dot(p, vm, preferred_element_type=F32) / l)
        o = jnp.where(head_lanes(outs[0].shape)[0], outs[0], outs[1])
        o_ref[bb, S:S + NM, :] = o.astype(BF16)


def _attention(rpb_flat, q, k, v):
    blk = pl.BlockSpec((None, ATT_BATCH, LP, LANES), lambda hp, b: (hp, b, 0, 0))
    return pl.pallas_call(
        _attn_kernel,
        grid=(HEAD_PAIRS, B // ATT_BATCH),
        in_specs=[pl.BlockSpec(memory_space=pltpu.SMEM), blk, blk, blk],
        out_specs=blk,
        out_shape=jax.ShapeDtypeStruct((HEAD_PAIRS, B, LP, LANES), BF16),
        scratch_shapes=[
            pltpu.VMEM((2, 2 * WIN_H - 1, GW, LANES), F32),
            pltpu.VMEM((2, 3, QB, KSLOTS), F32),
            pltpu.VMEM((QR, GW, LANES), F32),
        ] + [pltpu.VMEM((len(COL_GROUPS), KSLOTS, LANES), BF16)] * 4
          + [pltpu.VMEM((2, QB, KSLOTS), F32)] * 2
          + [pltpu.VMEM((2, QB, KSLOTS), BF16)] * 2
          + [pltpu.VMEM((2, QB, LANES), F32)] * 4,
        compiler_params=pltpu.CompilerParams(
            dimension_semantics=("arbitrary", "arbitrary"), vmem_limit_bytes=VMEM_LIMIT),
        name="natten",
    )(rpb_flat, q, k, v)


def kernel(x, meta_tokens, norm_mix_g, norm_mlp_g, conv_w_in, conv_b_in, conv_w_dw, conv_b_dw, conv_ln_g, conv_ln_b, conv_w_out, conv_b_out, na_w_qkv, na_w_o, na_rpb, mlp_w1, mlp_w2, final_norm_g):
    assert x.shape == (B, S, D) and x.dtype == F32
    row = lambda a: a.reshape(1, -1).astype(F32)
    w_in, w_out = conv_w_in.astype(BF16), conv_w_out.astype(BF16)
    w_qkv, w_o = na_w_qkv.astype(BF16), na_w_o.astype(BF16)
    w1, w2 = mlp_w1.astype(BF16), mlp_w2.astype(BF16)
    meta = meta_tokens.astype(F32)
    h = None
    for i in range(DEPTH):
        j = i // 2
        last = i == DEPTH - 1
        if i % 2 == 0:
            g, b_in = row(norm_mix_g[i]), row(conv_b_in[j])
            if i == 0:
                u = _conv_in_first(x, meta, g, w_in, b_in, j)
                res, res_meta = x, meta
            else:
                u = _conv_in(h, g, w_in, b_in, j).reshape(B, LP * FEAT_TILES, LANES)
                res, res_meta = h.reshape(B, LP, D), None
            h = _conv_out(
                u, res, res_meta,
                conv_w_dw[j].reshape(CONV_W, SUBLANES, LANES),
                conv_b_dw[j].reshape(SUBLANES, LANES),
                row(conv_ln_g[j]), row(conv_ln_b[j]), w_out, row(conv_b_out[j]), j).reshape(NT, D)
            proj = None
        else:
            q, k, v = _qkv(h, row(norm_mix_g[i]), w_qkv, j)
            slabs = lambda a: a.reshape(HEAD_PAIRS, B, LP, LANES)
            o = _attention(na_rpb[j].reshape(-1), slabs(q), slabs(k), slabs(v))
            proj = (o if last else o.reshape(HEAD_PAIRS, NT, LANES), w_o, j)
        h = _mlp(h.reshape(B, LP, D) if last else h, row(norm_mlp_g[i]), w1, w2, i,
                 proj=proj, final_g=row(final_norm_g) if last else None)
    return h
```
